```python
import jax
import jax.numpy as jnp
from jax import lax
import numpy as np

D_MODEL = 1024
BATCH = 16
SEQ = 4096
DEPTH = 1

D_FF = 2816
MACARON_WEIGHT = 0.5
NORM_EPS = 1e-6
CONV_WIDTH = 4
MLSTM_HEADS = 4
MLSTM_HEAD_DIM = 128
MLSTM_WIDTH = MLSTM_HEADS * MLSTM_HEAD_DIM
MLSTM_CHUNK = 64
NSA_HEADS = 8
NSA_KV_GROUPS = 2
NSA_HEADS_PER_GROUP = NSA_HEADS // NSA_KV_GROUPS
NSA_HEAD_DIM = 64
NSA_WIDTH = NSA_HEADS * NSA_HEAD_DIM
NSA_KV_WIDTH = NSA_KV_GROUPS * NSA_HEAD_DIM
CMP_BLOCK = 32
CMP_STRIDE = 16
CMP_HIDDEN = 256
SEL_BLOCK = 64
N_SELECT = 16
WINDOW = 256
NSA_QUERY_BLOCK = 32
FORCE_SCORE = 1e4
NEG_BIG = -1e30
IN_SPLITS = (MLSTM_WIDTH, MLSTM_WIDTH, MLSTM_WIDTH, MLSTM_WIDTH, 2 * MLSTM_HEADS,
             NSA_WIDTH, NSA_KV_WIDTH, NSA_KV_WIDTH, NSA_KV_WIDTH, NSA_KV_WIDTH, NSA_KV_WIDTH, NSA_KV_WIDTH,
             3 * NSA_HEADS, D_MODEL, D_MODEL)
D_IN = sum(IN_SPLITS)

kernel_name = "hybrid_mlstm_nsa_macaron_block"


def rmsnorm(x, gain):
    x32 = x.astype(jnp.float32)
    y = x32 * lax.rsqrt(jnp.mean(x32 * x32, axis=-1, keepdims=True) + NORM_EPS)
    return (y * gain.astype(jnp.float32)).astype(x.dtype)


def swiglu(u, wg, wu, wd):
    return (jax.nn.silu(u @ wg) * (u @ wu)) @ wd


def sandwich(h, fn, ada, s, g_pre, g_post, resid):
    shift, scale, gate = ada[:, s, 0, None, :], ada[:, s, 1, None, :], ada[:, s, 2, None, :]
    u = rmsnorm(h, g_pre) * (1 + scale) + shift
    return h + resid * gate * rmsnorm(fn(u), g_post)


def causal_depthwise_conv(x, w, b):
    y = lax.conv_general_dilated(x, w, window_strides=(1,), padding=[(CONV_WIDTH - 1, 0)],
                                 dimension_numbers=('NWC', 'WIO', 'NWC'),
                                 feature_group_count=x.shape[-1])
    return y + b


def masked_softmax(s, mask):
    s = jnp.where(mask, s.astype(jnp.float32), NEG_BIG)
    m = jnp.max(s, axis=-1, keepdims=True)
    e = jnp.where(mask, jnp.exp(s - m), 0.0)
    return e / jnp.maximum(jnp.sum(e, axis=-1, keepdims=True), 1e-30)


def mlstm_chunkwise(q, k, v, ig, logf):
    B, H, S, dh = q.shape
    L = MLSTM_CHUNK
    nc = S // L

    def chunks(t):
        return jnp.moveaxis(t.reshape(t.shape[:2] + (nc, L) + t.shape[3:]), 2, 0)

    causal = jnp.tril(jnp.ones((L, L), dtype=bool))

    def step(carry, xs):
        C, n, m = carry
        qc, kc, vc, ic, fc = xs
        b = jnp.cumsum(fc, axis=-1)
        log_d = jnp.where(causal, b[..., :, None] - b[..., None, :] + ic[..., None, :], -jnp.inf)
        a = b + m[..., None]
        m_t = jnp.maximum(a, jnp.max(log_d, axis=-1))
        s = jnp.einsum('bhtd,bhsd->bhts', qc, kc) * jnp.exp(log_d - m_t[..., None])
        w_inter = jnp.exp(a - m_t)
        num = jnp.einsum('bhts,bhsd->bhtd', s, vc) + w_inter[..., None] * jnp.einsum('bhtd,bhde->bhte', qc, C)
        den = jnp.sum(s, axis=-1) + w_inter * jnp.einsum('bhtd,bhd->bht', qc, n)
        h = num / jnp.maximum(jnp.abs(den), jnp.exp(-m_t))[..., None]
        b_last = b[..., -1]
        g = b_last[..., None] - b + ic
        m_new = jnp.maximum(b_last + m, jnp.max(g, axis=-1))
        decay = jnp.exp(b_last + m - m_new)
        w = jnp.exp(g - m_new[..., None])
        C_new = decay[..., None, None] * C + jnp.einsum('bhs,bhsd,bhse->bhde', w, kc, vc)
        n_new = decay[..., None] * n + jnp.einsum('bhs,bhsd->bhd', w, kc)
        return (C_new, n_new, m_new), h

    init = (jnp.zeros((B, H, dh, dh), jnp.float32), jnp.zeros((B, H, dh), jnp.float32),
            jnp.zeros((B, H), jnp.float32))
    _, hs = lax.scan(step, init, (chunks(q), chunks(k), chunks(v), chunks(ig), chunks(logf)))
    return jnp.moveaxis(hs, 0, 2).reshape(B, H, S, dh)


def mlstm_mixer(q, k, v, o_pre, if_pre, conv_w, conv_b, gate_b, head_gain):
    B, S, _ = q.shape
    qk = jax.nn.silu(causal_depthwise_conv(jnp.concatenate([q, k], axis=-1), conv_w, conv_b))
    q, k = jnp.split(qk, 2, axis=-1)

    def heads(t):
        return t.reshape(B, S, MLSTM_HEADS, MLSTM_HEAD_DIM).transpose(0, 2, 1, 3).astype(jnp.float32)

    gates = (if_pre + gate_b).astype(jnp.float32)
    ig = gates[..., :MLSTM_HEADS].transpose(0, 2, 1)
    logf = jax.nn.log_sigmoid(gates[..., MLSTM_HEADS:]).transpose(0, 2, 1)
    h = mlstm_chunkwise(heads(q), heads(k) * (MLSTM_HEAD_DIM ** -0.5), heads(v), ig, logf)
    h = h.transpose(0, 2, 1, 3)
    h = h * lax.rsqrt(jnp.mean(h * h, axis=-1, keepdims=True) + NORM_EPS)
    h = h.reshape(B, S, MLSTM_WIDTH) * head_gain.astype(jnp.float32)
    return (jax.nn.sigmoid(o_pre.astype(jnp.float32)) * h).astype(q.dtype)


def compress(k, pe, w1, b1, w2, b2):
    B, S, G, dh = k.shape
    r = CMP_BLOCK // CMP_STRIDE
    nch = S // CMP_STRIDE
    ch = k.reshape(B, nch, CMP_STRIDE, G, dh)
    blocks = jnp.concatenate([ch[:, j:nch - r + 1 + j] for j in range(r)], axis=2)
    blocks = blocks + pe[:, None, :]
    n_cmp = blocks.shape[1]
    flat = blocks.transpose(0, 1, 3, 2, 4).reshape(B, n_cmp, G, CMP_BLOCK * dh)
    hid = jax.nn.silu(flat @ w1 + b1)
    return (hid @ w2 + b2).transpose(0, 2, 1, 3)


def nsa_mixer(q, kc, vc, ks, vs, kw, vw, g, cmp_pe, cmp_w1, cmp_b1, cmp_w2, cmp_b2):
    B, S, _ = q.shape
    G, HG, dh, QB = NSA_KV_GROUPS, NSA_HEADS_PER_GROUP, NSA_HEAD_DIM, NSA_QUERY_BLOCK
    qh = q.reshape(B, S, G, HG, dh).transpose(0, 2, 3, 1, 4) * (dh ** -0.5)
    gates = jax.nn.sigmoid(g.astype(jnp.float32)).reshape(B, S, G, HG, 3).transpose(0, 2, 3, 1, 4)

    def kv(t):
        return t.reshape(B, S, G, dh)

    k_cmp = compress(kv(kc), cmp_pe[0], cmp_w1[0], cmp_b1[0], cmp_w2[0], cmp_b2[0])
    v_cmp = compress(kv(vc), cmp_pe[1], cmp_w1[1], cmp_b1[1], cmp_w2[1], cmp_b2[1])
    n_cmp = k_cmp.shape[2]
    cmp_start = jnp.arange(n_cmp) * CMP_STRIDE
    cmp_end = cmp_start + CMP_BLOCK - 1
    n_sel = S // SEL_BLOCK
    n_top = min(N_SELECT, n_sel)
    sel_start = jnp.arange(n_sel) * SEL_BLOCK
    overlap = ((cmp_start[:, None] < sel_start[None, :] + SEL_BLOCK)
               & (cmp_start[:, None] + CMP_BLOCK > sel_start[None, :])).astype(jnp.float32)

    def sel_blocks(t):
        return kv(t).reshape(B, n_sel, SEL_BLOCK, G, dh).transpose(0, 3, 1, 2, 4)

    k_sel, v_sel = sel_blocks(ks), sel_blocks(vs)

    def win_pad(t):
        return jnp.pad(kv(t).transpose(0, 2, 1, 3), ((0, 0), (0, 0), (WINDOW, 0), (0, 0)))

    k_win, v_win = win_pad(kw), win_pad(vw)
    b_idx = jnp.arange(B)[:, None, None, None]
    g_idx = jnp.arange(G)[None, :, None, None]
    sel_offsets = jnp.arange(SEL_BLOCK)
    win_offsets = jnp.arange(WINDOW + QB)
    blk = jnp.arange(n_sel)

    def query_block(qi):
        q0 = qi * QB
        t = q0 + jnp.arange(QB)
        qb = lax.dynamic_slice_in_dim(qh, q0, QB, axis=3)
        gb = lax.dynamic_slice_in_dim(gates, q0, QB, axis=3)
        p_cmp = masked_softmax(jnp.einsum('bghqd,bgnd->bghqn', qb, k_cmp), cmp_end[None, :] <= t[:, None])
        o_cmp = jnp.einsum('bghqn,bgnd->bghqd', p_cmp, v_cmp)
        imp = jnp.einsum('bghqn,nj->bgqj', p_cmp, overlap)
        cur = t // SEL_BLOCK
        forced = (blk[None, :] == 0) | (blk[None, :] == cur[:, None]) | (blk[None, :] == cur[:, None] - 1)
        causal_blk = blk[None, :] * SEL_BLOCK <= t[:, None]
        imp = jnp.where(causal_blk, jnp.where(forced, FORCE_SCORE, imp), -1.0)
        _, idx = lax.top_k(imp, n_top)
        k_g = k_sel[b_idx, g_idx, idx].reshape(B, G, QB, n_top * SEL_BLOCK, dh)
        v_g = v_sel[b_idx, g_idx, idx].reshape(B, G, QB, n_top * SEL_BLOCK, dh)
        pos = (idx[..., None] * SEL_BLOCK + sel_offsets).reshape(B, G, QB, n_top * SEL_BLOCK)
        p_sel = masked_softmax(jnp.einsum('bghqd,bgqnd->bghqn', qb, k_g), (pos <= t[:, None])[:, :, None])
        o_sel = jnp.einsum('bghqn,bgqnd->bghqd', p_sel, v_g)
        k_w = lax.dynamic_slice_in_dim(k_win, q0, WINDOW + QB, axis=2)
        v_w = lax.dynamic_slice_in_dim(v_win, q0, WINDOW + QB, axis=2)
        pos_w = q0 - WINDOW + win_offsets
        mask_w = (pos_w[None, :] >= 0) & (pos_w[None, :] <= t[:, None]) & (pos_w[None, :] > t[:, None] - WINDOW)
        p_w = masked_softmax(jnp.einsum('bghqd,bgnd->bghqn', qb, k_w), mask_w)
        o_w = jnp.einsum('bghqn,bgnd->bghqd', p_w, v_w)
        return gb[..., 0:1] * o_cmp + gb[..., 1:2] * o_sel + gb[..., 2:3] * o_w

    out = lax.map(query_block, jnp.arange(S // QB))
    return out.transpose(1, 0, 4, 2, 3, 5).reshape(B, S, NSA_WIDTH)


def token_mixer(u, w_in, conv_w, conv_b, gate_b, head_gain, cmp_pe, cmp_w1, cmp_b1, cmp_w2, cmp_b2,
                w_up_m, w_up_n, w_out):
    proj = u @ w_in
    offsets = [int(o) for o in np.cumsum(IN_SPLITS)[:-1]]
    (q_m, k_m, v_m, o_m, if_m, q_n, kc, vc, ks, vs, kw, vw, g_n, gate_m, gate_n) = jnp.split(proj, offsets, axis=-1)
    y_m = mlstm_mixer(q_m, k_m, v_m, o_m, if_m, conv_w, conv_b, gate_b, head_gain) @ w_up_m
    y_n = nsa_mixer(q_n, kc, vc, ks, vs, kw, vw, g_n, cmp_pe, cmp_w1, cmp_b1, cmp_w2, cmp_b2).astype(u.dtype) @ w_up_n
    merged = jax.nn.sigmoid(gate_m) * y_m + jax.nn.sigmoid(gate_n) * y_n
    return merged @ w_out


def setup_inputs(seed: int = 0) -> dict:
    key = jax.random.key(seed)
    ks = jax.random.split(key, 24)

    def nrm(k, shape, scale):
        return jax.random.normal(k, shape, jnp.float32) * scale

    L, D, F = DEPTH, D_MODEL, D_FF
    gate_b = jnp.concatenate([nrm(ks[10], (L, MLSTM_HEADS), 0.1),
                              jnp.linspace(3.0, 6.0, MLSTM_HEADS)[None, :] + nrm(ks[11], (L, MLSTM_HEADS), 0.1)], axis=-1)
    return {
        "x": nrm(ks[0], (BATCH, SEQ, D), 1.0),
        "c": nrm(ks[1], (BATCH, D), 1.0),
        "w_ada": nrm(ks[2], (L, D, 9 * D), D ** -0.5),
        "b_ada": nrm(ks[3], (L, 9 * D), 0.02),
        "norm_gains": 1.0 + nrm(ks[4], (L, 6, D), 0.05),
        "ffn_wg": nrm(ks[5], (L, 2, D, F), D ** -0.5),
        "ffn_wu": nrm(ks[6], (L, 2, D, F), D ** -0.5),
        "ffn_wd": nrm(ks[7], (L, 2, F, D), F ** -0.5),
        "w_in": nrm(ks[8], (L, D, D_IN), D ** -0.5),
        "mlstm_conv_w": nrm(ks[9], (L, CONV_WIDTH, 1, 2 * MLSTM_WIDTH), CONV_WIDTH ** -0.5),
        "mlstm_conv_b": nrm(ks[12], (L, 2 * MLSTM_WIDTH), 0.02),
        "mlstm_gate_b": gate_b,
        "mlstm_head_gain": 1.0 + nrm(ks[13], (L, MLSTM_WIDTH), 0.05),
        "cmp_pe": nrm(ks[14], (L, 2, CMP_BLOCK, NSA_HEAD_DIM), 0.1),
        "cmp_w1": nrm(ks[15], (L, 2, CMP_BLOCK * NSA_HEAD_DIM, CMP_HIDDEN), (CMP_BLOCK * NSA_HEAD_DIM) ** -0.5),
        "cmp_b1": nrm(ks[16], (L, 2, CMP_HIDDEN), 0.02),
        "cmp_w2": nrm(ks[17], (L, 2, CMP_HIDDEN, NSA_HEAD_DIM), CMP_HIDDEN ** -0.5),
        "cmp_b2": nrm(ks[18], (L, 2, NSA_HEAD_DIM), 0.02),
        "w_up_mlstm": nrm(ks[19], (L, MLSTM_WIDTH, D), MLSTM_WIDTH ** -0.5),
        "w_up_nsa": nrm(ks[20], (L, NSA_WIDTH, D), NSA_WIDTH ** -0.5),
        "w_out": nrm(ks[21], (L, D, D), D ** -0.5),
    }


def reference(x, c, w_ada, b_ada, norm_gains, ffn_wg, ffn_wu, ffn_wd, w_in, mlstm_conv_w, mlstm_conv_b,
              mlstm_gate_b, mlstm_head_gain, cmp_pe, cmp_w1, cmp_b1, cmp_w2, cmp_b2, w_up_mlstm, w_up_nsa, w_out):
    B = x.shape[0]
    h = x
    for l in range(DEPTH):
        ada = (jax.nn.silu(c) @ w_ada[l] + b_ada[l]).reshape(B, 3, 3, D_MODEL)
        h = sandwich(h, lambda u: swiglu(u, ffn_wg[l, 0], ffn_wu[l, 0], ffn_wd[l, 0]), ada, 0,
                     norm_gains[l, 0], norm_gains[l, 1], MACARON_WEIGHT)
        h = sandwich(h, lambda u: token_mixer(u, w_in[l], mlstm_conv_w[l], mlstm_conv_b[l], mlstm_gate_b[l],
                                               mlstm_head_gain[l], cmp_pe[l], cmp_w1[l], cmp_b1[l], cmp_w2[l],
                                               cmp_b2[l], w_up_mlstm[l], w_up_nsa[l], w_out[l]),
                     ada, 1, norm_gains[l, 2], norm_gains[l, 3], 1.0)
        h = sandwich(h, lambda u: swiglu(u, ffn_wg[l, 1], ffn_wu[l, 1], ffn_wd[l, 1]), ada, 2,
                     norm_gains[l, 4], norm_gains[l, 5], MACARON_WEIGHT)
    return h.astype(x.dtype)
```

```python
import functools

import jax
import jax.numpy as jnp
from jax import lax
from jax.experimental import pallas as pl
from jax.experimental.pallas import tpu as pltpu

D_MODEL = 1024
D_FF = 2816
NORM_EPS = 1e-6
CONV_WIDTH = 4
MLSTM_HEADS = 4
MLSTM_HEAD_DIM = 128
MLSTM_WIDTH = MLSTM_HEADS * MLSTM_HEAD_DIM
NSA_HEADS = 8
NSA_KV_GROUPS = 2
NSA_HPG = NSA_HEADS // NSA_KV_GROUPS
NSA_HEAD_DIM = 64
NSA_WIDTH = NSA_HEADS * NSA_HEAD_DIM
NSA_KV_WIDTH = NSA_KV_GROUPS * NSA_HEAD_DIM
CMP_BLOCK = 32
CMP_STRIDE = 16
CMP_HIDDEN = 256
SEL_BLOCK = 64
N_SELECT = 16
WINDOW = 256
FORCE_SCORE = 1e4
NEG_BIG = -1e30
M_INIT = -1e29

LANES = 128
VMEM_LIMIT = 56 * 1024 * 1024

FFN_TM = 512
FFN_FC = 256
MLSTM_L = 256
NSA_TQ = 128
NSA_TK = 256

BF16 = jnp.bfloat16
F32 = jnp.float32


def _params(sem):
    return pltpu.CompilerParams(dimension_semantics=sem, vmem_limit_bytes=VMEM_LIMIT)


def _rms(y):
    return y * lax.rsqrt(jnp.mean(y * y, axis=-1, keepdims=True) + NORM_EPS)


def _sigmoid(x):
    return 1.0 / (1.0 + jnp.exp(-x))


def _dot(a, b):
    return jnp.dot(a, b, preferred_element_type=F32)


def _dot_nt(a, b, precision=None):
    return lax.dot_general(a, b, (((1,), (1,)), ((), ())), preferred_element_type=F32, precision=precision)


def _dot_tn(a, b):
    return lax.dot_general(a, b, (((0,), (0,)), ((), ())), preferred_element_type=F32)


def _ada_kernel(c_ref, w_ref, b_ref, o_ref):
    c = c_ref[...]
    sc = c * _sigmoid(c)
    o_ref[...] = jnp.dot(sc, w_ref[...], preferred_element_type=F32,
                         precision=lax.Precision.HIGHEST) + b_ref[...]


def _ada(c, w, b):
    bsz, d = c.shape
    n = w.shape[1]
    tn = 1152
    return pl.pallas_call(
        _ada_kernel,
        grid=(n // tn,),
        in_specs=[pl.BlockSpec((bsz, d), lambda j: (0, 0)),
                  pl.BlockSpec((d, tn), lambda j: (0, j)),
                  pl.BlockSpec((1, tn), lambda j: (0, j))],
        out_specs=pl.BlockSpec((bsz, tn), lambda j: (0, j)),
        out_shape=jax.ShapeDtypeStruct((bsz, n), F32),
        compiler_params=_params(("arbitrary",)),
        name="ada",
    )(c, w, b.reshape(1, n))


def _mod_rows(ada_ref, s):
    return (ada_ref[0, 3 * s:3 * s + 1, :], ada_ref[0, 3 * s + 1:3 * s + 2, :],
            ada_ref[0, 3 * s + 2:3 * s + 3, :])


def _ffn_kernel(h_ref, ada_ref, gains_ref, wg_ref, wu_ref, wd_ref, o_ref, *, s, resid):
    h = h_ref[...]
    shift, scale, gate = _mod_rows(ada_ref, s)
    u = (_rms(h) * gains_ref[2 * s:2 * s + 1, :] * (1.0 + scale) + shift).astype(BF16)
    acc = jnp.zeros(h.shape, F32)
    for c0 in range(0, D_FF, FFN_FC):
        g = _dot(u, wg_ref[:, c0:c0 + FFN_FC])
        up = _dot(u, wu_ref[:, c0:c0 + FFN_FC])
        hid = (g * _sigmoid(g) * up).astype(BF16)
        acc = acc + _dot(hid, wd_ref[c0:c0 + FFN_FC, :])
    post = _rms(acc) * gains_ref[2 * s + 1:2 * s + 2, :]
    o_ref[...] = h + resid * gate * post


def _ffn(h, ada3, gains, wg, wu, wd, s, resid, seq):
    t, d = h.shape
    tm = FFN_TM
    per_b = seq // tm
    const = lambda i: (0, 0)
    return pl.pallas_call(
        functools.partial(_ffn_kernel, s=s, resid=resid),
        grid=(t // tm,),
        in_specs=[pl.BlockSpec((tm, d), lambda i: (i, 0)),
                  pl.BlockSpec((1, 9, d), lambda i: (i // per_b, 0, 0)),
                  pl.BlockSpec(gains.shape, const),
                  pl.BlockSpec(wg.shape, const, pipeline_mode=pl.Buffered(1)),
                  pl.BlockSpec(wu.shape, const, pipeline_mode=pl.Buffered(1)),
                  pl.BlockSpec(wd.shape, const, pipeline_mode=pl.Buffered(1))],
        out_specs=pl.BlockSpec((tm, d), lambda i: (i, 0)),
        out_shape=jax.ShapeDtypeStruct((t, d), F32),
        compiler_params=_params(("parallel",)),
        name=f"ffn{s}",
    )(h, ada3, gains, wg, wu, wd)


_QK_W = 2 * MLSTM_WIDTH
_VO_W = 2 * MLSTM_WIDTH
_QN_W = NSA_WIDTH
_KV_W = 6 * NSA_KV_WIDTH
_GT_W = 2 * D_MODEL
_SM_W = LANES
_IF_COLS = 2 * MLSTM_HEADS
_GN_COLS = 3 * NSA_HEADS
_IN_OFF = (0, _QK_W, _QK_W + _VO_W, _QK_W + _VO_W + _QN_W, _QK_W + _VO_W + _QN_W + _KV_W,
           _QK_W + _VO_W + _QN_W + _KV_W + _GT_W)
_IN_TOTAL = _IN_OFF[-1] + _SM_W


def _inproj_kernel(h_ref, ada_ref, gains_ref, w_ref, qk_ref, vo_ref, qn_ref, kv_ref, gt_ref, sm_ref):
    h = h_ref[...]
    shift, scale, _ = _mod_rows(ada_ref, 1)
    u = (_rms(h) * gains_ref[2:3, :] * (1.0 + scale) + shift).astype(BF16)
    step = 512
    for ref, off in ((qk_ref, _IN_OFF[0]), (vo_ref, _IN_OFF[1]), (qn_ref, _IN_OFF[2]),
                     (kv_ref, _IN_OFF[3]), (gt_ref, _IN_OFF[4]), (sm_ref, _IN_OFF[5])):
        width = ref.shape[1]
        for c0 in range(0, width, step):
            c1 = min(c0 + step, width)
            ref[:, c0:c1] = _dot(u, w_ref[:, off + c0:off + c1]).astype(ref.dtype)


def _inproj(h, ada3, gains, w, seq):
    t, d = h.shape
    tm = FFN_TM
    per_b = seq // tm
    const = lambda i: (0, 0)
    widths = (_QK_W, _VO_W, _QN_W, _KV_W, _GT_W, _SM_W)
    dtypes = (BF16, BF16, BF16, BF16, BF16, F32)
    return pl.pallas_call(
        _inproj_kernel,
        grid=(t // tm,),
        in_specs=[pl.BlockSpec((tm, d), lambda i: (i, 0)),
                  pl.BlockSpec((1, 9, d), lambda i: (i // per_b, 0, 0)),
                  pl.BlockSpec(gains.shape, const),
                  pl.BlockSpec(w.shape, const, pipeline_mode=pl.Buffered(1))],
        out_specs=[pl.BlockSpec((tm, wd), lambda i: (i, 0)) for wd in widths],
        out_shape=[jax.ShapeDtypeStruct((t, wd), dt) for wd, dt in zip(widths, dtypes)],
        compiler_params=_params(("parallel",)),
        name="inproj",
    )(h, ada3, gains, w)


def _mlstm_kernel(qk_ref, vo_ref, sm_ref, cw_ref, cb_ref, gb_ref, hg_ref, o_ref, xbuf, cext, mscr):
    L = qk_ref.shape[0]
    W = MLSTM_WIDTH
    dh = MLSTM_HEAD_DIM
    H = MLSTM_HEADS

    @pl.when(pl.program_id(1) == 0)
    def _():
        xbuf[0:8, :] = jnp.zeros((8, 2 * W), F32)
        cext[...] = jnp.zeros(cext.shape, F32)
        mscr[...] = jnp.zeros(mscr.shape, F32)

    x = qk_ref[...].astype(F32)
    xbuf[8:8 + L, :] = x
    conv = jnp.broadcast_to(cb_ref[...], (L, 2 * W))
    for j in range(CONV_WIDTH):
        conv = conv + cw_ref[j:j + 1, :] * xbuf[pl.ds(8 - (CONV_WIDTH - 1) + j, L), :]
    xbuf[0:8, :] = x[L - 8:L, :]
    qk = conv * _sigmoid(conv)

    gts = sm_ref[...] + gb_ref[...]
    logf = jnp.minimum(gts, 0.0) - jnp.log1p(jnp.exp(-jnp.abs(gts)))
    row = lax.broadcasted_iota(jnp.int32, (L, L), 0)
    col = lax.broadcasted_iota(jnp.int32, (L, L), 1)
    causal = row >= col
    bmat = jnp.dot(causal.astype(F32), logf, preferred_element_type=F32,
                   precision=lax.Precision.HIGHEST)
    g_t = gts.T
    b_t = bmat.T
    ones_col = (lax.broadcasted_iota(jnp.int32, (L, dh), 1) == 0).astype(BF16)

    for h in range(H):
        q = qk[:, h * dh:(h + 1) * dh].astype(BF16)
        k32 = qk[:, W + h * dh:W + (h + 1) * dh] * (dh ** -0.5)
        k = k32.astype(BF16)
        vext = jnp.concatenate([vo_ref[:, h * dh:(h + 1) * dh], ones_col], axis=1)
        b_col = bmat[:, H + h:H + h + 1]
        i_col = gts[:, h:h + 1]
        r_row = g_t[h:h + 1, :] - b_t[H + h:H + h + 1, :]
        m_prev = mscr[h:h + 1, 0:1]
        c_old = cext[h]

        ld = jnp.where(causal, b_col + r_row, NEG_BIG)
        a = b_col + m_prev
        m_t = jnp.maximum(a, jnp.max(ld, axis=-1, keepdims=True))
        sd = (_dot_nt(q, k) * jnp.exp(ld - m_t)).astype(BF16)
        w_inter = jnp.exp(a - m_t)
        nd = _dot(sd, vext) + w_inter * _dot(q, c_old.astype(BF16))
        hh = nd[:, :dh] / jnp.maximum(jnp.abs(nd[:, dh:dh + 1]), jnp.exp(-m_t))

        b_last = b_col[L - 1:L, :]
        g_col = b_last - b_col + i_col
        m_new = jnp.maximum(b_last + m_prev, jnp.max(g_col, axis=0, keepdims=True))
        kw = (k32 * jnp.exp(g_col - m_new)).astype(BF16)
        cext[h] = jnp.exp(b_last + m_prev - m_new) * c_old + _dot_tn(kw, vext)
        mscr[h:h + 1, :] = jnp.broadcast_to(m_new, (1, LANES))

        hn = _rms(hh) * hg_ref[:, h * dh:(h + 1) * dh]
        o_pre = vo_ref[:, W + h * dh:W + (h + 1) * dh].astype(F32)
        o_ref[:, h * dh:(h + 1) * dh] = (_sigmoid(o_pre) * hn).astype(o_ref.dtype)


def _mlstm(qk, vo, sm, conv_w, conv_b, gate_b_row, head_gain, bsz, seq):
    t = qk.shape[0]
    L = min(MLSTM_L, seq)
    nc = seq // L
    W = MLSTM_WIDTH
    const = lambda b, c: (0, 0)
    rows = lambda b, c: (b * nc + c, 0)
    return pl.pallas_call(
        _mlstm_kernel,
        grid=(bsz, nc),
        in_specs=[pl.BlockSpec((L, 2 * W), rows),
                  pl.BlockSpec((L, 2 * W), rows),
                  pl.BlockSpec((L, LANES), rows),
                  pl.BlockSpec(conv_w.shape, const),
                  pl.BlockSpec(conv_b.shape, const),
                  pl.BlockSpec(gate_b_row.shape, const),
                  pl.BlockSpec(head_gain.shape, const)],
        out_specs=pl.BlockSpec((L, W), rows),
        out_shape=jax.ShapeDtypeStruct((t, W), BF16),
        scratch_shapes=[pltpu.VMEM((L + 8, 2 * W), F32),
                        pltpu.VMEM((MLSTM_HEADS, MLSTM_HEAD_DIM, 2 * MLSTM_HEAD_DIM), F32),
                        pltpu.VMEM((8, LANES), F32)],
        compiler_params=_params(("parallel", "arbitrary")),
        name="mlstm",
    )(qk, vo, sm, conv_w, conv_b, gate_b_row, head_gain)


def _compress_kernel(kc_ref, vc_ref, wa_ref, wb_ref, pea_ref, peb_ref, b1_ref, w2_ref, b2_ref, ko_ref, vo_ref):
    for i, (x_ref, o_ref) in enumerate(((kc_ref, ko_ref), (vc_ref, vo_ref))):
        x = x_ref[0]
        n = x.shape[0]
        wa, wb = wa_ref[i], wb_ref[i]
        bias = _dot(pea_ref[i], wa) + _dot(peb_ref[i], wb) + b1_ref[i]
        pre = _dot(x, wa) + pltpu.roll(_dot(x, wb), n - 1, 0) + bias[0:1, :]
        hid = (pre * _sigmoid(pre)).astype(BF16)
        o_ref[0] = (_dot(hid, w2_ref[i]) + b2_ref[i]).astype(o_ref.dtype)


def _compress(kc, vc, wa, wb, pea, peb, b1, w2, b2):
    bsz, n, width = kc.shape
    full = lambda a: pl.BlockSpec(a.shape, lambda b: (0,) * a.ndim)
    per_b = pl.BlockSpec((1, n, width), lambda b: (b, 0, 0))
    out = pl.BlockSpec((1, n, NSA_KV_WIDTH), lambda b: (b, 0, 0))
    return pl.pallas_call(
        _compress_kernel,
        grid=(bsz,),
        in_specs=[per_b, per_b, full(wa), full(wb), full(pea), full(peb), full(b1), full(w2), full(b2)],
        out_specs=[out, out],
        out_shape=[jax.ShapeDtypeStruct((bsz, n, NSA_KV_WIDTH), BF16)] * 2,
        compiler_params=_params(("parallel",)),
        name="compress",
    )(kc, vc, wa, wb, pea, peb, b1, w2, b2)


def _softmax_rows(s):
    m = jnp.maximum(jnp.max(s, axis=-1, keepdims=True), M_INIT)
    e = jnp.exp(s - m)
    return e, jnp.maximum(jnp.sum(e, axis=-1, keepdims=True), 1e-30)


def _nsa_kernel(qn_ref, sm_ref, kc_ref, vc_ref, ks_ref, vs_ref, kw_ref, vw_ref, pin_ref, pout_ref, o_ref,
                imp_scr, sel_scr):
    TQ = qn_ref.shape[0]
    S = ks_ref.shape[0]
    NC = kc_ref.shape[1]
    TK = min(NSA_TK, S)
    NSEL = S // SEL_BLOCK
    HG = NSA_HPG
    R = HG * TQ
    WL = min(WINDOW + TQ, S)
    q0 = pl.program_id(1) * TQ

    q_all = _dot(qn_ref[...], pin_ref[...]).astype(BF16)
    gates = _sigmoid(sm_ref[...])
    t_col = q0 + lax.broadcasted_iota(jnp.int32, (TQ, 1), 0)
    t_row = q0 + lax.broadcasted_iota(jnp.int32, (1, TQ), 1)

    def stack(f):
        return jnp.concatenate([f(h) for h in range(HG)], axis=0)

    outs = []
    for g in range(NSA_KV_GROUPS):
        q = stack(lambda h: q_all[:, (g * HG + h) * LANES:(g * HG + h + 1) * LANES])
        gate = lambda br: stack(lambda h: gates[:, _IF_COLS + (g * HG + h) * 3 + br:
                                                _IF_COLS + (g * HG + h) * 3 + br + 1])

        cmp_end = lax.broadcasted_iota(jnp.int32, (1, NC), 1) * CMP_STRIDE + (CMP_BLOCK - 1)
        bias_c = jnp.where(cmp_end <= t_col, 0.0, NEG_BIG)
        s = _dot_nt(q, kc_ref[0]).reshape(HG, TQ, NC) + bias_c[None]
        e, l = _softmax_rows(s)
        p = e / l
        o_cmp = _dot(p.reshape(R, NC).astype(BF16), vc_ref[0])

        psum = jnp.sum(p, axis=0)
        jj = lax.broadcasted_iota(jnp.int32, (NSEL, NC), 0) * SEL_BLOCK
        nn = lax.broadcasted_iota(jnp.int32, (NSEL, NC), 1) * CMP_STRIDE
        overlap_t = ((nn < jj + SEL_BLOCK) & (nn + CMP_BLOCK > jj)).astype(F32)
        imp = _dot_nt(overlap_t, psum, precision=lax.Precision.HIGHEST)
        blk = lax.broadcasted_iota(jnp.int32, (NSEL, TQ), 0)
        cur = t_row // SEL_BLOCK
        forced = (blk == 0) | (blk == cur) | (blk == cur - 1)
        imp = jnp.where(blk * SEL_BLOCK <= t_row, jnp.where(forced, FORCE_SCORE, imp), -1.0)
        imp_scr[...] = imp

        def rank_body(i, cnt):
            r = imp_scr[pl.ds(i, 1), :]
            ahead = (r > imp) | ((r == imp) & (i < blk))
            return cnt + jnp.where(ahead, 1.0, 0.0)

        cnt = lax.fori_loop(0, NSEL, rank_body, jnp.zeros((NSEL, TQ), F32))
        sel_scr[...] = jnp.where(cnt < float(min(N_SELECT, NSEL)), 1.0, 0.0).astype(BF16)

        def sel_body(kt, carry):
            m, l, acc = carry
            k0 = pl.multiple_of(kt * TK, TK)
            kpos = k0 + lax.broadcasted_iota(jnp.int32, (1, TK), 1)
            kblk = k0 // SEL_BLOCK + lax.broadcasted_iota(jnp.int32, (NSEL, TK), 1) // SEL_BLOCK
            expand = (lax.broadcasted_iota(jnp.int32, (NSEL, TK), 0) == kblk).astype(BF16)
            picked = _dot_tn(sel_scr[...], expand)
            bias = jnp.where((picked > 0.5) & (kpos <= t_col), 0.0, NEG_BIG)
            s = _dot_nt(q, ks_ref[pl.ds(k0, TK), :]).reshape(HG, TQ, TK) + bias[None]
            s = s.reshape(R, TK)
            m_new = jnp.maximum(m, jnp.max(s, axis=-1, keepdims=True))
            alpha = jnp.exp(m - m_new)
            e = jnp.exp(s - m_new)
            l = alpha * l + jnp.sum(e, axis=-1, keepdims=True)
            acc = alpha * acc + _dot(e.astype(BF16), vs_ref[pl.ds(k0, TK), :])
            return m_new, l, acc

        n_kt = (q0 + TQ + TK - 1) // TK
        _, l, acc = lax.fori_loop(0, n_kt, sel_body,
                                  (jnp.full((R, 1), M_INIT, F32), jnp.zeros((R, 1), F32),
                                   jnp.zeros((R, LANES), F32)))
        o_sel = acc / jnp.maximum(l, 1e-30)

        w0 = pl.multiple_of(jnp.maximum(q0 + TQ - WL, 0), TQ)
        wpos = w0 + lax.broadcasted_iota(jnp.int32, (1, WL), 1)
        bias_w = jnp.where((wpos <= t_col) & (wpos > t_col - WINDOW), 0.0, NEG_BIG)
        s = _dot_nt(q, kw_ref[pl.ds(w0, WL), :]).reshape(HG, TQ, WL) + bias_w[None]
        e, l = _softmax_rows(s)
        o_win = _dot((e / l).reshape(R, WL).astype(BF16), vw_ref[pl.ds(w0, WL), :])

        o = gate(0) * o_cmp + gate(1) * o_sel + gate(2) * o_win
        outs += [o[h * TQ:(h + 1) * TQ, :].astype(BF16) for h in range(HG)]

    o_ref[...] = _dot(jnp.concatenate(outs, axis=1), pout_ref[...]).astype(o_ref.dtype)


def _nsa(qn, sm, k_cmp, v_cmp, kv, pin, pout, bsz, seq):
    t = qn.shape[0]
    tq = min(NSA_TQ, seq)
    nq = seq // tq
    nc = k_cmp.shape[1]
    rows = lambda b, i: (b * nq + i, 0)
    const = lambda b, i: (0, 0)
    piece = lambda j: pl.BlockSpec((seq, NSA_KV_WIDTH), lambda b, i: (b, j))
    cmp_spec = pl.BlockSpec((1, nc, NSA_KV_WIDTH), lambda b, i: (b, 0, 0))
    return pl.pallas_call(
        _nsa_kernel,
        grid=(bsz, nq),
        in_specs=[pl.BlockSpec((tq, NSA_WIDTH), rows),
                  pl.BlockSpec((tq, LANES), rows),
                  cmp_spec, cmp_spec, piece(2), piece(3), piece(4), piece(5),
                  pl.BlockSpec(pin.shape, const), pl.BlockSpec(pout.shape, const)],
        out_specs=pl.BlockSpec((tq, NSA_WIDTH), rows),
        out_shape=jax.ShapeDtypeStruct((t, NSA_WIDTH), BF16),
        scratch_shapes=[pltpu.VMEM((seq // SEL_BLOCK, tq), F32),
                        pltpu.VMEM((seq // SEL_BLOCK, tq), BF16)],
        compiler_params=_params(("parallel", "arbitrary")),
        name="nsa",
    )(qn, sm, k_cmp, v_cmp, kv, kv, kv, kv, pin, pout)


def _merge_kernel(h_ref, ada_ref, gains_ref, hm_ref, hn_ref, gt_ref, wm_ref, wn_ref, wo_ref, o_ref):
    d = h_ref.shape[1]
    h = h_ref[...]
    _, _, gate = _mod_rows(ada_ref, 1)
    y_m = _dot(hm_ref[...], wm_ref[...])
    y_n = _dot(hn_ref[...], wn_ref[...])
    merged = (_sigmoid(gt_ref[:, :d].astype(F32)) * y_m + _sigmoid(gt_ref[:, d:].astype(F32)) * y_n)
    z = _dot(merged.astype(BF16), wo_ref[...])
    o_ref[...] = h + gate * (_rms(z) * gains_ref[3:4, :])


def _merge(h, ada3, gains, hm, hn, gt, wm, wn, wo, seq):
    t, d = h.shape
    tm = FFN_TM
    per_b = seq // tm
    const = lambda i: (0, 0)
    rows = lambda i: (i, 0)
    return pl.pallas_call(
        _merge_kernel,
        grid=(t // tm,),
        in_specs=[pl.BlockSpec((tm, d), rows),
                  pl.BlockSpec((1, 9, d), lambda i: (i // per_b, 0, 0)),
                  pl.BlockSpec(gains.shape, const),
                  pl.BlockSpec((tm, hm.shape[1]), rows),
                  pl.BlockSpec((tm, hn.shape[1]), rows),
                  pl.BlockSpec((tm, gt.shape[1]), rows),
                  pl.BlockSpec(wm.shape, const), pl.BlockSpec(wn.shape, const), pl.BlockSpec(wo.shape, const)],
        out_specs=pl.BlockSpec((tm, d), rows),
        out_shape=jax.ShapeDtypeStruct((t, d), F32),
        compiler_params=_params(("parallel",)),
        name="merge",
    )(h, ada3, gains, hm, hn, gt, wm, wn, wo)


def _reorder_w_in(w_in):
    m, hq = MLSTM_WIDTH, 2 * MLSTM_HEADS
    o_if = 4 * m
    o_qn = o_if + hq
    o_kv = o_qn + NSA_WIDTH
    o_gn = o_kv + 6 * NSA_KV_WIDTH
    o_gm = o_gn + 3 * NSA_HEADS
    small = jnp.concatenate([w_in[:, o_if:o_qn], w_in[:, o_gn:o_gm],
                             jnp.zeros((w_in.shape[0], _SM_W - _IF_COLS - _GN_COLS), w_in.dtype)], axis=1)
    return jnp.concatenate([w_in[:, :o_if], w_in[:, o_qn:o_gn], w_in[:, o_gm:], small], axis=1).astype(BF16)


def _compress_weights(cmp_pe, cmp_w1, cmp_b1, cmp_w2, cmp_b2):
    G, dh, hid = NSA_KV_GROUPS, NSA_HEAD_DIM, CMP_HIDDEN
    eye = jnp.eye(G, dtype=F32)
    w1 = cmp_w1.reshape(2, CMP_BLOCK, dh, hid)
    w1 = jnp.einsum('kpdj,gh->kpgdhj', w1, eye).reshape(2, CMP_BLOCK * G * dh, G * hid)
    half = CMP_STRIDE * G * dh
    wa, wb = w1[:, :half].astype(BF16), w1[:, half:].astype(BF16)
    pe = jnp.broadcast_to(cmp_pe[:, :, None, :], (2, CMP_BLOCK, G, dh)).reshape(2, 1, CMP_BLOCK * G * dh)
    pe = jnp.broadcast_to(pe, (2, 8, CMP_BLOCK * G * dh)).astype(BF16)
    pea, peb = pe[:, :, :half], pe[:, :, half:]
    b1 = jnp.tile(cmp_b1, (1, G)).reshape(2, 1, G * hid)
    w2 = jnp.einsum('kjd,gh->kgjhd', cmp_w2, eye).reshape(2, G * hid, G * dh).astype(BF16)
    b2 = jnp.tile(cmp_b2, (1, G)).reshape(2, 1, G * dh)
    return wa, wb, pea, peb, b1, w2, b2


def _placement_matrices():
    dh = NSA_HEAD_DIM
    src = jnp.arange(NSA_WIDTH)
    head, dim = src // dh, src % dh
    dst = head * LANES + (head // NSA_HPG) * dh + dim
    pin = jnp.zeros((NSA_WIDTH, NSA_HEADS * LANES), F32).at[src, dst].set(dh ** -0.5)
    pout = jnp.zeros((NSA_HEADS * LANES, NSA_WIDTH), F32).at[dst, src].set(1.0)
    return pin.astype(BF16), pout.astype(BF16)


def kernel(x, c, w_ada, b_ada, norm_gains, ffn_wg, ffn_wu, ffn_wd, w_in, mlstm_conv_w, mlstm_conv_b,
           mlstm_gate_b, mlstm_head_gain, cmp_pe, cmp_w1, cmp_b1, cmp_w2, cmp_b2, w_up_mlstm, w_up_nsa, w_out):
    bsz, seq, d = x.shape
    t = bsz * seq
    h = x.reshape(t, d)
    pin, pout = _placement_matrices()
    for l in range(w_ada.shape[0]):
        gains = norm_gains[l]
        ada3 = _ada(c, w_ada[l], b_ada[l]).reshape(bsz, 9, d)
        wg, wu, wd = ffn_wg[l].astype(BF16), ffn_wu[l].astype(BF16), ffn_wd[l].astype(BF16)

        h = _ffn(h, ada3, gains, wg[0], wu[0], wd[0], 0, 0.5, seq)

        qk, vo, qn, kv, gt, sm = _inproj(h, ada3, gains, _reorder_w_in(w_in[l]), seq)
        gate_b_row = jnp.concatenate([mlstm_gate_b[l], jnp.zeros((LANES - _IF_COLS,), F32)]).reshape(1, LANES)
        hm = _mlstm(qk, vo, sm, mlstm_conv_w[l].reshape(CONV_WIDTH, 2 * MLSTM_WIDTH),
                    mlstm_conv_b[l].reshape(1, -1), gate_b_row, mlstm_head_gain[l].reshape(1, -1), bsz, seq)

        n_chunks = seq // CMP_STRIDE
        kc = kv[:, 0:NSA_KV_WIDTH].reshape(bsz, n_chunks, CMP_STRIDE * NSA_KV_WIDTH)
        vc = kv[:, NSA_KV_WIDTH:2 * NSA_KV_WIDTH].reshape(bsz, n_chunks, CMP_STRIDE * NSA_KV_WIDTH)
        k_cmp, v_cmp = _compress(kc, vc, *_compress_weights(cmp_pe[l], cmp_w1[l], cmp_b1[l], cmp_w2[l], cmp_b2[l]))
        hn = _nsa(qn, sm, k_cmp, v_cmp, kv, pin, pout, bsz, seq)

        h = _merge(h, ada3, gains, hm, hn, gt, w_up_mlstm[l].astype(BF16), w_up_nsa[l].astype(BF16),
                   w_out[l].astype(BF16), seq)
        h = _ffn(h, ada3, gains, wg[1], wu[1], wd[1], 2, 0.5, seq)
    return h.reshape(bsz, seq, d)
```

```python
import functools

import jax
import jax.numpy as jnp
from jax import lax
from jax.experimental import pallas as pl
from jax.experimental.pallas import tpu as pltpu

D_MODEL = 1024
D_FF = 2816
NORM_EPS = 1e-6
CONV_WIDTH = 4
MLSTM_HEADS = 4
MLSTM_HEAD_DIM = 128
MLSTM_WIDTH = MLSTM_HEADS * MLSTM_HEAD_DIM
NSA_HEADS = 8
NSA_KV_GROUPS = 2
NSA_HPG = NSA_HEADS // NSA_KV_GROUPS
NSA_HEAD_DIM = 64
NSA_WIDTH = NSA_HEADS * NSA_HEAD_DIM
NSA_KV_WIDTH = NSA_KV_GROUPS * NSA_HEAD_DIM
CMP_BLOCK = 32
CMP_STRIDE = 16
CMP_HIDDEN = 256
SEL_BLOCK = 64
N_SELECT = 16
WINDOW = 256
FORCE_SCORE = 1e4
NEG_BIG = -1e30
M_INIT = -1e29

LANES = 128
SUBLANES = 8
VMEM_LIMIT = 56 * 1024 * 1024

FFN_TM = 512
FFN_FC = 256
MLSTM_L = 256
NSA_TQ = 256
NSA_TK = 256

BF16 = jnp.bfloat16
F32 = jnp.float32


def _params(sem):
    return pltpu.CompilerParams(dimension_semantics=sem, vmem_limit_bytes=VMEM_LIMIT)


def _rms(y):
    return y * lax.rsqrt(jnp.mean(y * y, axis=-1, keepdims=True) + NORM_EPS)


def _sigmoid(x):
    return 1.0 / (1.0 + jnp.exp(-x))


def _dot(a, b):
    return jnp.dot(a, b, preferred_element_type=F32)


def _dot_nt(a, b, precision=None):
    return lax.dot_general(a, b, (((1,), (1,)), ((), ())), preferred_element_type=F32, precision=precision)


def _dot_tn(a, b):
    return lax.dot_general(a, b, (((0,), (0,)), ((), ())), preferred_element_type=F32)


def _ada_kernel(c_ref, w_ref, b_ref, o_ref):
    c = c_ref[...]
    sc = c * _sigmoid(c)
    o_ref[...] = jnp.dot(sc, w_ref[...], preferred_element_type=F32,
                         precision=lax.Precision.HIGHEST) + b_ref[...]


def _ada(c, w, b):
    bsz, d = c.shape
    n = w.shape[1]
    tn = 1152
    return pl.pallas_call(
        _ada_kernel,
        grid=(n // tn,),
        in_specs=[pl.BlockSpec((bsz, d), lambda j: (0, 0)),
                  pl.BlockSpec((d, tn), lambda j: (0, j)),
                  pl.BlockSpec((1, tn), lambda j: (0, j))],
        out_specs=pl.BlockSpec((bsz, tn), lambda j: (0, j)),
        out_shape=jax.ShapeDtypeStruct((bsz, n), F32),
        compiler_params=_params(("arbitrary",)),
        name="ada",
    )(c, w, b.reshape(1, n))


def _mod_rows(ada_ref, s):
    return (ada_ref[0, 3 * s:3 * s + 1, :], ada_ref[0, 3 * s + 1:3 * s + 2, :],
            ada_ref[0, 3 * s + 2:3 * s + 3, :])


def _ffn_kernel(h_ref, ada_ref, gains_ref, wg_ref, wu_ref, wd_ref, o_ref, *, s, resid):
    h = h_ref[...]
    shift, scale, gate = _mod_rows(ada_ref, s)
    u = (_rms(h) * gains_ref[2 * s:2 * s + 1, :] * (1.0 + scale) + shift).astype(BF16)
    acc = jnp.zeros(h.shape, F32)
    for c0 in range(0, D_FF, FFN_FC):
        g = _dot(u, wg_ref[:, c0:c0 + FFN_FC])
        up = _dot(u, wu_ref[:, c0:c0 + FFN_FC])
        hid = (g * _sigmoid(g) * up).astype(BF16)
        acc = acc + _dot(hid, wd_ref[c0:c0 + FFN_FC, :])
    post = _rms(acc) * gains_ref[2 * s + 1:2 * s + 2, :]
    o_ref[...] = h + resid * gate * post


def _ffn(h, ada3, gains, wg, wu, wd, s, resid, seq):
    t, d = h.shape
    tm = FFN_TM
    per_b = seq // tm
    const = lambda i: (0, 0)
    return pl.pallas_call(
        functools.partial(_ffn_kernel, s=s, resid=resid),
        grid=(t // tm,),
        in_specs=[pl.BlockSpec((tm, d), lambda i: (i, 0)),
                  pl.BlockSpec((1, 9, d), lambda i: (i // per_b, 0, 0)),
                  pl.BlockSpec(gains.shape, const),
                  pl.BlockSpec(wg.shape, const, pipeline_mode=pl.Buffered(1)),
                  pl.BlockSpec(wu.shape, const, pipeline_mode=pl.Buffered(1)),
                  pl.BlockSpec(wd.shape, const, pipeline_mode=pl.Buffered(1))],
        out_specs=pl.BlockSpec((tm, d), lambda i: (i, 0)),
        out_shape=jax.ShapeDtypeStruct((t, d), F32),
        compiler_params=_params(("parallel",)),
        name=f"ffn{s}",
    )(h, ada3, gains, wg, wu, wd)


_QK_W = 2 * MLSTM_WIDTH
_VO_W = 2 * MLSTM_WIDTH
_QN_W = NSA_WIDTH
_KC_W = NSA_KV_WIDTH
_VC_W = NSA_KV_WIDTH
_KK_W = 3 * LANES
_GT_W = 2 * D_MODEL
_SM_W = LANES
_VT_W = 4 * LANES
_IF_COLS = 2 * MLSTM_HEADS
_GN_COLS = 3 * NSA_HEADS
_IN_WIDTHS = (_QK_W, _VO_W, _QN_W, _KC_W, _VC_W, _KK_W, _GT_W, _SM_W)
_IN_DTYPES = (BF16, BF16, BF16, BF16, BF16, BF16, BF16, F32)


def _inproj_kernel(h_ref, ada_ref, gains_ref, w_ref, wt_ref, qk_ref, vo_ref, qn_ref, kc_ref, vc_ref, kk_ref,
                   gt_ref, sm_ref, vt_ref, *, per_b):
    tm = h_ref.shape[0]
    h = h_ref[...]
    shift, scale, _ = _mod_rows(ada_ref, 1)
    u = (_rms(h) * gains_ref[2:3, :] * (1.0 + scale) + shift).astype(BF16)
    step = 512
    off = 0
    for ref in (qk_ref, vo_ref, qn_ref, kc_ref, vc_ref, kk_ref, gt_ref, sm_ref):
        width = ref.shape[1]
        for c0 in range(0, width, step):
            c1 = min(c0 + step, width)
            res = _dot(u, w_ref[:, off + c0:off + c1])
            if ref is kk_ref and c0 == 0:
                pos = (pl.program_id(0) % per_b) * tm + lax.broadcasted_iota(jnp.int32, (tm, c1 - c0), 0)
                lane = lax.broadcasted_iota(jnp.int32, (tm, c1 - c0), 1)
                hot = (lane == NSA_HEAD_DIM + pos // SEL_BLOCK) | (lane == LANES + pos // SEL_BLOCK)
                res = res + jnp.where(hot, 1.0, 0.0)
            ref[:, c0:c1] = res.astype(ref.dtype)
        off += width
    ones_row = lax.broadcasted_iota(jnp.int32, (_VT_W, tm), 0) % LANES == NSA_HEAD_DIM
    vt_ref[...] = (_dot_nt(wt_ref[...], u) + jnp.where(ones_row, 1.0, 0.0)).astype(vt_ref.dtype)


def _inproj(h, ada3, gains, w, wt, seq):
    t, d = h.shape
    tm = FFN_TM
    per_b = seq // tm
    const = lambda i: (0, 0)
    return pl.pallas_call(
        functools.partial(_inproj_kernel, per_b=per_b),
        grid=(t // tm,),
        in_specs=[pl.BlockSpec((tm, d), lambda i: (i, 0)),
                  pl.BlockSpec((1, 9, d), lambda i: (i // per_b, 0, 0)),
                  pl.BlockSpec(gains.shape, const),
                  pl.BlockSpec(w.shape, const, pipeline_mode=pl.Buffered(1)),
                  pl.BlockSpec(wt.shape, const, pipeline_mode=pl.Buffered(1))],
        out_specs=[pl.BlockSpec((tm, wd), lambda i: (i, 0)) for wd in _IN_WIDTHS]
        + [pl.BlockSpec((_VT_W, tm), lambda i: (0, i))],
        out_shape=[jax.ShapeDtypeStruct((t, wd), dt) for wd, dt in zip(_IN_WIDTHS, _IN_DTYPES)]
        + [jax.ShapeDtypeStruct((_VT_W, t), BF16)],
        compiler_params=_params(("parallel",)),
        name="inproj",
    )(h, ada3, gains, w, wt)


def _mlstm_kernel(qk_ref, vo_ref, sm_ref, cw_ref, cb_ref, gb_ref, hg_ref, o_ref, xbuf, cext, mscr):
    L = qk_ref.shape[0]
    W = MLSTM_WIDTH
    dh = MLSTM_HEAD_DIM
    H = MLSTM_HEADS

    @pl.when(pl.program_id(1) == 0)
    def _():
        xbuf[0:8, :] = jnp.zeros((8, 2 * W), F32)
        cext[...] = jnp.zeros(cext.shape, F32)
        mscr[...] = jnp.zeros(mscr.shape, F32)

    x = qk_ref[...].astype(F32)
    xbuf[8:8 + L, :] = x
    conv = jnp.broadcast_to(cb_ref[...], (L, 2 * W))
    for j in range(CONV_WIDTH):
        conv = conv + cw_ref[j:j + 1, :] * xbuf[pl.ds(8 - (CONV_WIDTH - 1) + j, L), :]
    xbuf[0:8, :] = x[L - 8:L, :]
    qk = conv * _sigmoid(conv)

    gts = sm_ref[...] + gb_ref[...]
    logf = jnp.minimum(gts, 0.0) - jnp.log1p(jnp.exp(-jnp.abs(gts)))
    row = lax.broadcasted_iota(jnp.int32, (L, L), 0)
    col = lax.broadcasted_iota(jnp.int32, (L, L), 1)
    causal = row >= col
    bmat = jnp.dot(causal.astype(F32), logf, preferred_element_type=F32,
                   precision=lax.Precision.HIGHEST)
    g_t = gts.T
    b_t = bmat.T
    ones_col = (lax.broadcasted_iota(jnp.int32, (L, dh), 1) == 0).astype(BF16)

    for h in range(H):
        q = qk[:, h * dh:(h + 1) * dh].astype(BF16)
        k32 = qk[:, W + h * dh:W + (h + 1) * dh] * (dh ** -0.5)
        k = k32.astype(BF16)
        vext = jnp.concatenate([vo_ref[:, h * dh:(h + 1) * dh], ones_col], axis=1)
        b_col = bmat[:, H + h:H + h + 1]
        i_col = gts[:, h:h + 1]
        r_row = g_t[h:h + 1, :] - b_t[H + h:H + h + 1, :]
        m_prev = mscr[h:h + 1, 0:1]
        c_old = cext[h]

        ld = jnp.where(causal, b_col + r_row, NEG_BIG)
        a = b_col + m_prev
        m_t = jnp.maximum(a, jnp.max(ld, axis=-1, keepdims=True))
        sd = (_dot_nt(q, k) * jnp.exp(ld - m_t)).astype(BF16)
        w_inter = jnp.exp(a - m_t)
        nd = _dot(sd, vext) + w_inter * _dot(q, c_old.astype(BF16))
        hh = nd[:, :dh] / jnp.maximum(jnp.abs(nd[:, dh:dh + 1]), jnp.exp(-m_t))

        b_last = b_col[L - 1:L, :]
        g_col = b_last - b_col + i_col
        m_new = jnp.maximum(b_last + m_prev, jnp.max(g_col, axis=0, keepdims=True))
        kw = (k32 * jnp.exp(g_col - m_new)).astype(BF16)
        cext[h] = jnp.exp(b_last + m_prev - m_new) * c_old + _dot_tn(kw, vext)
        mscr[h:h + 1, :] = jnp.broadcast_to(m_new, (1, LANES))

        hn = _rms(hh) * hg_ref[:, h * dh:(h + 1) * dh]
        o_pre = vo_ref[:, W + h * dh:W + (h + 1) * dh].astype(F32)
        o_ref[:, h * dh:(h + 1) * dh] = (_sigmoid(o_pre) * hn).astype(o_ref.dtype)


def _mlstm(qk, vo, sm, conv_w, conv_b, gate_b_row, head_gain, bsz, seq):
    t = qk.shape[0]
    L = min(MLSTM_L, seq)
    nc = seq // L
    W = MLSTM_WIDTH
    const = lambda b, c: (0, 0)
    rows = lambda b, c: (b * nc + c, 0)
    return pl.pallas_call(
        _mlstm_kernel,
        grid=(bsz, nc),
        in_specs=[pl.BlockSpec((L, 2 * W), rows),
                  pl.BlockSpec((L, 2 * W), rows),
                  pl.BlockSpec((L, LANES), rows),
                  pl.BlockSpec(conv_w.shape, const),
                  pl.BlockSpec(conv_b.shape, const),
                  pl.BlockSpec(gate_b_row.shape, const),
                  pl.BlockSpec(head_gain.shape, const)],
        out_specs=pl.BlockSpec((L, W), rows),
        out_shape=jax.ShapeDtypeStruct((t, W), BF16),
        scratch_shapes=[pltpu.VMEM((L + 8, 2 * W), F32),
                        pltpu.VMEM((MLSTM_HEADS, MLSTM_HEAD_DIM, 2 * MLSTM_HEAD_DIM), F32),
                        pltpu.VMEM((8, LANES), F32)],
        compiler_params=_params(("parallel", "arbitrary")),
        name="mlstm",
    )(qk, vo, sm, conv_w, conv_b, gate_b_row, head_gain)


def _compress_kernel(kc_ref, vc_ref, wa_ref, wb_ref, pea_ref, peb_ref, b1_ref, w2k_ref, b2k_ref, w2vt_ref, b2vt_ref,
                     ko_ref, vto_ref):
    def hidden(i, x):
        wa, wb = wa_ref[i], wb_ref[i]
        bias = _dot(pea_ref[i], wa) + _dot(peb_ref[i], wb) + b1_ref[i]
        pre = _dot(x, wa) + pltpu.roll(_dot(x, wb), x.shape[0] - 1, 0) + bias[0:1, :]
        return (pre * _sigmoid(pre)).astype(BF16)

    ko_ref[0] = (_dot(hidden(0, kc_ref[0]), w2k_ref[...]) + b2k_ref[...]).astype(ko_ref.dtype)
    vto_ref[0] = (_dot_nt(w2vt_ref[...], hidden(1, vc_ref[0])) + b2vt_ref[...]).astype(vto_ref.dtype)


def _compress(kc, vc, wa, wb, pea, peb, b1, w2k, b2k, w2vt, b2vt):
    bsz, n, width = kc.shape
    full = lambda a: pl.BlockSpec(a.shape, lambda b: (0,) * a.ndim)
    per_b = pl.BlockSpec((1, n, width), lambda b: (b, 0, 0))
    return pl.pallas_call(
        _compress_kernel,
        grid=(bsz,),
        in_specs=[per_b, per_b] + [full(a) for a in (wa, wb, pea, peb, b1, w2k, b2k, w2vt, b2vt)],
        out_specs=[pl.BlockSpec((1, n, NSA_KV_WIDTH), lambda b: (b, 0, 0)),
                   pl.BlockSpec((1, NSA_KV_WIDTH, n), lambda b: (b, 0, 0))],
        out_shape=[jax.ShapeDtypeStruct((bsz, n, NSA_KV_WIDTH), BF16),
                   jax.ShapeDtypeStruct((bsz, NSA_KV_WIDTH, n), BF16)],
        compiler_params=_params(("parallel",)),
        name="compress",
    )(kc, vc, wa, wb, pea, peb, b1, w2k, b2k, w2vt, b2vt)


def _softmax_cols(s):
    m = jnp.maximum(jnp.max(s, axis=0, keepdims=True), M_INIT)
    e = jnp.exp(s - m)
    return e * (1.0 / jnp.maximum(jnp.sum(e, axis=0, keepdims=True), 1e-30))


def _nsa_kernel(qn_ref, sm_ref, kc_ref, vct_ref, ks0_ref, ks1_ref, kw_ref, vs0_ref, vs1_ref, vw0_ref, vw1_ref,
                scl_ref, eye_ref, o_ref, imp_scr, cnt_scr, m_scr, den_scr, num_scr):
    TQ = qn_ref.shape[0]
    S = kw_ref.shape[0]
    NC = kc_ref.shape[1]
    TK = min(NSA_TK, S)
    NSEL = S // SEL_BLOCK
    NB = NSEL // SUBLANES
    HG = NSA_HPG
    DH = NSA_HEAD_DIM
    R = HG * TQ
    WL = min(WINDOW + TQ, S)
    q0 = pl.program_id(1) * TQ

    qt_all = _dot_nt(scl_ref[...], qn_ref[...]).astype(BF16)
    gates_t = _sigmoid(sm_ref[...]).T
    t_lane = q0 + lax.broadcasted_iota(jnp.int32, (1, TQ), 1)
    zeros_half = jnp.zeros((DH, R), BF16)
    sub = lax.broadcasted_iota(jnp.int32, (SUBLANES, TQ), 0)

    def lanes(f):
        return jnp.concatenate([f(h) for h in range(HG)], axis=1)

    def per_head(x):
        return jnp.concatenate([x] * HG, axis=1)

    def weights(s):
        m = jnp.maximum(jnp.max(s, axis=0, keepdims=True), M_INIT)
        return m, jnp.exp(s - m)

    def value_sum(v_ref, k0, n, e):
        nd = _dot(v_ref[:, pl.ds(k0, n)], e.astype(BF16))
        return nd[0:DH, :], nd[DH:DH + 1, :]

    groups = []
    for g, (ks_ref, vw_ref) in enumerate(((ks0_ref, vw0_ref), (ks1_ref, vw1_ref))):
        qt = lanes(lambda h: qt_all[(g * HG + h) * DH:(g * HG + h + 1) * DH, :])
        with_half = lambda other: jnp.concatenate([qt, other] if g == 0 else [other, qt], axis=0)
        qt_plain = with_half(zeros_half)
        gate = lambda br: lanes(lambda h: gates_t[_IF_COLS + (g * HG + h) * 3 + br:
                                                  _IF_COLS + (g * HG + h) * 3 + br + 1, :])

        cmp_end = lax.broadcasted_iota(jnp.int32, (NC, TQ), 0) * CMP_STRIDE + (CMP_BLOCK - 1)
        bias_c = jnp.where(cmp_end <= t_lane, 0.0, NEG_BIG)
        p = _softmax_cols(_dot(kc_ref[0], qt_plain) + per_head(bias_c))
        o_cmp = _dot(vct_ref[0], p.astype(BF16))[g * DH:(g + 1) * DH, :]

        psum = p[:, 0:TQ]
        for h in range(1, HG):
            psum = psum + p[:, h * TQ:(h + 1) * TQ]
        jj = lax.broadcasted_iota(jnp.int32, (NSEL, NC), 0) * SEL_BLOCK
        nn = lax.broadcasted_iota(jnp.int32, (NSEL, NC), 1) * CMP_STRIDE
        overlap = ((nn < jj + SEL_BLOCK) & (nn + CMP_BLOCK > jj)).astype(F32)
        imp = jnp.dot(overlap, psum, preferred_element_type=F32, precision=lax.Precision.HIGHEST)
        blk = lax.broadcasted_iota(jnp.int32, (NSEL, TQ), 0)
        cur = t_lane // SEL_BLOCK
        forced = (blk == 0) | (blk == cur) | (blk == cur - 1)
        imp = jnp.where(blk * SEL_BLOCK <= t_lane, jnp.where(forced, FORCE_SCORE, imp), -1.0)
        imp_scr[...] = imp
        cnt_scr[...] = jnp.zeros((NSEL, TQ), F32)
        for ib in range(NB):
            @pl.when(ib * SUBLANES * SEL_BLOCK < q0 + TQ)
            def _():
                cnt = [cnt_scr[jb * SUBLANES:(jb + 1) * SUBLANES, :] for jb in range(NB)]
                for u in range(SUBLANES):
                    r = imp_scr[ib * SUBLANES + u:ib * SUBLANES + u + 1, :]
                    for jb in range(NB):
                        x = imp[jb * SUBLANES:(jb + 1) * SUBLANES, :]
                        if jb > ib:
                            ahead = jnp.where(r >= x, 1.0, 0.0)
                        elif jb < ib:
                            ahead = jnp.where(r > x, 1.0, 0.0)
                        else:
                            ahead = jnp.where(r > x, 1.0, jnp.where((r == x) & (sub > u), 1.0, 0.0))
                        cnt[jb] = cnt[jb] + ahead
                for jb in range(NB):
                    cnt_scr[jb * SUBLANES:(jb + 1) * SUBLANES, :] = cnt[jb]

        selb = jnp.where(cnt_scr[...] < float(min(N_SELECT, NSEL)), 0.0, NEG_BIG)
        if NSEL < DH:
            selb = jnp.concatenate([selb, jnp.zeros((DH - NSEL, TQ), F32)], axis=0)
        qt_sel = with_half(per_head(selb.astype(BF16)))

        w0 = pl.multiple_of(jnp.maximum(q0 + TQ - WL, 0), LANES)
        wpos = w0 + lax.broadcasted_iota(jnp.int32, (WL, TQ), 0)
        bias_w = jnp.where((wpos <= t_lane) & (wpos > t_lane - WINDOW), 0.0, NEG_BIG)
        _, e = weights(_dot(kw_ref[pl.ds(w0, WL), :], qt_plain) + per_head(bias_w))
        num, den = value_sum(vw_ref, w0, WL, e)
        o_win = num * (1.0 / jnp.maximum(den, 1e-30))

        groups.append((qt_sel, gate(0) * o_cmp + gate(2) * o_win, gate(1)))

    def sel_scores(g, k0, diagonal):
        s = _dot((ks0_ref, ks1_ref)[g][pl.ds(k0, TK), :], groups[g][0])
        if diagonal:
            kpos = k0 + lax.broadcasted_iota(jnp.int32, (TK, TQ), 0)
            s = s + per_head(jnp.where(kpos <= t_lane, 0.0, NEG_BIG))
        return s

    def sel_fold(g, k0, s):
        m_t, e = weights(s)
        num_t, den_t = value_sum((vs0_ref, vs1_ref)[g], k0, TK, e)
        m = m_scr[g]
        m_new = jnp.maximum(m, m_t)
        a, b = jnp.exp(m - m_new), jnp.exp(m_t - m_new)
        m_scr[g] = m_new
        den_scr[g] = a * den_scr[g] + b * den_t
        num_scr[g] = a * num_scr[g] + b * num_t

    def sel_block(tiles):
        scores = [[sel_scores(g, k0, diagonal) for g in range(NSA_KV_GROUPS)] for k0, diagonal in tiles]
        for (k0, _), per_group in zip(tiles, scores):
            for g, s in enumerate(per_group):
                sel_fold(g, k0, s)

    m_scr[...] = jnp.full(m_scr.shape, M_INIT, F32)
    den_scr[...] = jnp.zeros(den_scr.shape, F32)
    num_scr[...] = jnp.zeros(num_scr.shape, F32)
    n_full = q0 // TK

    def pair_body(j, c):
        k0 = pl.multiple_of(j * (2 * TK), 2 * TK)
        sel_block([(k0, False), (k0 + TK, False)])
        return c

    lax.fori_loop(0, n_full // 2, pair_body, 0)
    k_diag = pl.multiple_of(n_full * TK, TK)

    @pl.when(n_full % 2 == 1)
    def _():
        sel_block([(k_diag - TK, False), (k_diag, True)])

    @pl.when(n_full % 2 == 0)
    def _():
        sel_block([(k_diag, True)])

    out_rows = []
    for g, (_, o_rest, gate_sel) in enumerate(groups):
        o = o_rest + gate_sel * (num_scr[g] * (1.0 / jnp.maximum(den_scr[g], 1e-30)))
        out_rows += [o[:, h * TQ:(h + 1) * TQ] for h in range(HG)]

    o_ref[...] = _dot_tn(jnp.concatenate(out_rows, axis=0).astype(BF16), eye_ref[...]).astype(o_ref.dtype)


def _nsa(qn, sm, k_cmp, v_cmp_t, kk, vt, scl, eye, bsz, seq):
    t = qn.shape[0]
    tq = min(NSA_TQ, seq)
    nq = seq // tq
    nc = k_cmp.shape[1]
    nsel = seq // SEL_BLOCK
    assert nsel <= NSA_HEAD_DIM and nsel % SUBLANES == 0 and seq % min(NSA_TK, seq) == 0
    rows = lambda b, i: (b * nq + i, 0)
    const = lambda b, i: (0, 0)
    key_piece = lambda j: pl.BlockSpec((seq, LANES), lambda b, i: (b, j))
    val_piece = lambda j: pl.BlockSpec((LANES, seq), lambda b, i: (j, b))
    return pl.pallas_call(
        _nsa_kernel,
        grid=(bsz, nq),
        in_specs=[pl.BlockSpec((tq, NSA_WIDTH), rows),
                  pl.BlockSpec((tq, LANES), rows),
                  pl.BlockSpec((1, nc, NSA_KV_WIDTH), lambda b, i: (b, 0, 0)),
                  pl.BlockSpec((1, NSA_KV_WIDTH, nc), lambda b, i: (b, 0, 0)),
                  key_piece(0), key_piece(1), key_piece(2),
                  val_piece(0), val_piece(1), val_piece(2), val_piece(3),
                  pl.BlockSpec(scl.shape, const), pl.BlockSpec(eye.shape, const)],
        out_specs=pl.BlockSpec((tq, NSA_WIDTH), rows),
        out_shape=jax.ShapeDtypeStruct((t, NSA_WIDTH), BF16),
        scratch_shapes=[pltpu.VMEM((nsel, tq), F32), pltpu.VMEM((nsel, tq), F32),
                        pltpu.VMEM((NSA_KV_GROUPS, 1, NSA_HPG * tq), F32),
                        pltpu.VMEM((NSA_KV_GROUPS, 1, NSA_HPG * tq), F32),
                        pltpu.VMEM((NSA_KV_GROUPS, NSA_HEAD_DIM, NSA_HPG * tq), F32)],
        compiler_params=_params(("parallel", "arbitrary")),
        name="nsa",
    )(qn, sm, k_cmp, v_cmp_t, kk, kk, kk, vt, vt, vt, vt, scl, eye)


def _merge_kernel(h_ref, ada_ref, gains_ref, hm_ref, hn_ref, gt_ref, wm_ref, wn_ref, wo_ref, o_ref):
    d = h_ref.shape[1]
    h = h_ref[...]
    _, _, gate = _mod_rows(ada_ref, 1)
    y_m = _dot(hm_ref[...], wm_ref[...])
    y_n = _dot(hn_ref[...], wn_ref[...])
    merged = (_sigmoid(gt_ref[:, :d].astype(F32)) * y_m + _sigmoid(gt_ref[:, d:].astype(F32)) * y_n)
    z = _dot(merged.astype(BF16), wo_ref[...])
    o_ref[...] = h + gate * (_rms(z) * gains_ref[3:4, :])


def _merge(h, ada3, gains, hm, hn, gt, wm, wn, wo, seq):
    t, d = h.shape
    tm = FFN_TM
    per_b = seq // tm
    const = lambda i: (0, 0)
    rows = lambda i: (i, 0)
    return pl.pallas_call(
        _merge_kernel,
        grid=(t // tm,),
        in_specs=[pl.BlockSpec((tm, d), rows),
                  pl.BlockSpec((1, 9, d), lambda i: (i // per_b, 0, 0)),
                  pl.BlockSpec(gains.shape, const),
                  pl.BlockSpec((tm, hm.shape[1]), rows),
                  pl.BlockSpec((tm, hn.shape[1]), rows),
                  pl.BlockSpec((tm, gt.shape[1]), rows),
                  pl.BlockSpec(wm.shape, const), pl.BlockSpec(wn.shape, const), pl.BlockSpec(wo.shape, const)],
        out_specs=pl.BlockSpec((tm, d), rows),
        out_shape=jax.ShapeDtypeStruct((t, d), F32),
        compiler_params=_params(("parallel",)),
        name="merge",
    )(h, ada3, gains, hm, hn, gt, wm, wn, wo)


def _reorder_w_in(w_in):
    m, kvw, dh = MLSTM_WIDTH, NSA_KV_WIDTH, NSA_HEAD_DIM
    o_if = 4 * m
    o_qn = o_if + _IF_COLS
    o_kc = o_qn + NSA_WIDTH
    o_vc, o_ks, o_vs, o_kw, o_vw = (o_kc + i * kvw for i in range(1, 6))
    o_gn = o_vw + kvw
    o_gm = o_gn + _GN_COLS
    zeros = lambda n: jnp.zeros((w_in.shape[0], n), w_in.dtype)
    w = jnp.concatenate([
        w_in[:, :o_if], w_in[:, o_qn:o_kc], w_in[:, o_kc:o_vc], w_in[:, o_vc:o_ks],
        w_in[:, o_ks:o_ks + dh], zeros(dh), zeros(dh), w_in[:, o_ks + dh:o_vs], w_in[:, o_kw:o_vw],
        w_in[:, o_gm:],
        w_in[:, o_if:o_qn], w_in[:, o_gn:o_gm], zeros(_SM_W - _IF_COLS - _GN_COLS)], axis=1)
    wt = jnp.concatenate([piece for o in (o_vs, o_vw) for g in range(NSA_KV_GROUPS)
                          for piece in (w_in[:, o + g * dh:o + (g + 1) * dh], zeros(LANES - dh))], axis=1).T
    return w.astype(BF16), wt.astype(BF16)


def _compress_weights(cmp_pe, cmp_w1, cmp_b1, cmp_w2, cmp_b2):
    G, dh, hid = NSA_KV_GROUPS, NSA_HEAD_DIM, CMP_HIDDEN
    eye = jnp.eye(G, dtype=F32)
    w1 = cmp_w1.reshape(2, CMP_BLOCK, dh, hid)
    w1 = jnp.einsum('kpdj,gh->kpgdhj', w1, eye).reshape(2, CMP_BLOCK * G * dh, G * hid)
    half = CMP_STRIDE * G * dh
    wa, wb = w1[:, :half].astype(BF16), w1[:, half:].astype(BF16)
    pe = jnp.broadcast_to(cmp_pe[:, :, None, :], (2, CMP_BLOCK, G, dh)).reshape(2, 1, CMP_BLOCK * G * dh)
    pe = jnp.broadcast_to(pe, (2, 8, CMP_BLOCK * G * dh)).astype(BF16)
    pea, peb = pe[:, :, :half], pe[:, :, half:]
    b1 = jnp.tile(cmp_b1, (1, G)).reshape(2, 1, G * hid)
    w2 = jnp.einsum('kjd,gh->kgjhd', cmp_w2, eye).reshape(2, G * hid, G * dh).astype(BF16)
    b2 = jnp.tile(cmp_b2, (1, G))
    return wa, wb, pea, peb, b1, w2[0], b2[0].reshape(1, G * dh), w2[1].T, b2[1].reshape(G * dh, 1)


def kernel(x, c, w_ada, b_ada, norm_gains, ffn_wg, ffn_wu, ffn_wd, w_in, mlstm_conv_w, mlstm_conv_b,
           mlstm_gate_b, mlstm_head_gain, cmp_pe, cmp_w1, cmp_b1, cmp_w2, cmp_b2, w_up_mlstm, w_up_nsa, w_out):
    bsz, seq, d = x.shape
    t = bsz * seq
    h = x.reshape(t, d)
    eye = jnp.eye(NSA_WIDTH, dtype=BF16)
    scl = (jnp.eye(NSA_WIDTH, dtype=F32) * NSA_HEAD_DIM ** -0.5).astype(BF16)
    for l in range(w_ada.shape[0]):
        gains = norm_gains[l]
        ada3 = _ada(c, w_ada[l], b_ada[l]).reshape(bsz, 9, d)
        wg, wu, wd = ffn_wg[l].astype(BF16), ffn_wu[l].astype(BF16), ffn_wd[l].astype(BF16)

        h = _ffn(h, ada3, gains, wg[0], wu[0], wd[0], 0, 0.5, seq)

        qk, vo, qn, kc, vc, kk, gt, sm, vt = _inproj(h, ada3, gains, *_reorder_w_in(w_in[l]), seq)
        gate_b_row = jnp.concatenate([mlstm_gate_b[l], jnp.zeros((LANES - _IF_COLS,), F32)]).reshape(1, LANES)
        hm = _mlstm(qk, vo, sm, mlstm_conv_w[l].reshape(CONV_WIDTH, 2 * MLSTM_WIDTH),
                    mlstm_conv_b[l].reshape(1, -1), gate_b_row, mlstm_head_gain[l].reshape(1, -1), bsz, seq)

        chunked = (bsz, seq // CMP_STRIDE, CMP_STRIDE * NSA_KV_WIDTH)
        k_cmp, v_cmp_t = _compress(kc.reshape(chunked), vc.reshape(chunked),
                                   *_compress_weights(cmp_pe[l], cmp_w1[l], cmp_b1[l], cmp_w2[l], cmp_b2[l]))
        hn = _nsa(qn, sm, k_cmp, v_cmp_t, kk, vt, scl, eye, bsz, seq)

        h = _merge(h, ada3, gains, hm, hn, gt, w_up_mlstm[l].astype(BF16), w_up_nsa[l].astype(BF16),
                   w_out[l].astype(BF16), seq)
        h = _ffn(h, ada3, gains, wg[1], wu[1], wd[1], 2, 0.5, seq)
    return h.reshape(bsz, seq, d)
```

```python
import functools

import jax
import jax.numpy as jnp
from jax import lax
from jax.experimental import pallas as pl
from jax.experimental.pallas import tpu as pltpu

D_MODEL = 1024
D_FF = 2816
NORM_EPS = 1e-6
CONV_WIDTH = 4
MLSTM_HEADS = 4
MLSTM_HEAD_DIM = 128
MLSTM_WIDTH = MLSTM_HEADS * MLSTM_HEAD_DIM
NSA_HEADS = 8
NSA_KV_GROUPS = 2
NSA_HPG = NSA_HEADS // NSA_KV_GROUPS
NSA_HEAD_DIM = 64
NSA_WIDTH = NSA_HEADS * NSA_HEAD_DIM
NSA_KV_WIDTH = NSA_KV_GROUPS * NSA_HEAD_DIM
CMP_BLOCK = 32
CMP_STRIDE = 16
CMP_HIDDEN = 256
SEL_BLOCK = 64
N_SELECT = 16
WINDOW = 256
FORCE_SCORE = 1e4
NEG_BIG = -1e30
M_INIT = -1e29

LANES = 128
SUBLANES = 8
VMEM_LIMIT = 56 * 1024 * 1024

FFN_TM = 512
FFN_FC = 256
MLSTM_L = 256
NSA_TQ = 256
NSA_TK = 256

BF16 = jnp.bfloat16
F32 = jnp.float32


def _params(sem):
    return pltpu.CompilerParams(dimension_semantics=sem, vmem_limit_bytes=VMEM_LIMIT)


def _rms(y):
    return y * lax.rsqrt(jnp.mean(y * y, axis=-1, keepdims=True) + NORM_EPS)


def _sigmoid(x):
    return 0.5 * jnp.tanh(0.5 * x) + 0.5


def _dot(a, b):
    return jnp.dot(a, b, preferred_element_type=F32)


def _dot_nt(a, b, precision=None):
    return lax.dot_general(a, b, (((1,), (1,)), ((), ())), preferred_element_type=F32, precision=precision)


def _dot_tn(a, b):
    return lax.dot_general(a, b, (((0,), (0,)), ((), ())), preferred_element_type=F32)


def _ada_kernel(c_ref, w_ref, b_ref, o_ref):
    c = c_ref[...]
    sc = c * _sigmoid(c)
    o_ref[...] = jnp.dot(sc, w_ref[...], preferred_element_type=F32,
                         precision=lax.Precision.HIGHEST) + b_ref[...]


def _ada(c, w, b):
    bsz, d = c.shape
    n = w.shape[1]
    tn = 1152
    return pl.pallas_call(
        _ada_kernel,
        grid=(n // tn,),
        in_specs=[pl.BlockSpec((bsz, d), lambda j: (0, 0)),
                  pl.BlockSpec((d, tn), lambda j: (0, j)),
                  pl.BlockSpec((1, tn), lambda j: (0, j))],
        out_specs=pl.BlockSpec((bsz, tn), lambda j: (0, j)),
        out_shape=jax.ShapeDtypeStruct((bsz, n), F32),
        compiler_params=_params(("arbitrary",)),
        name="ada",
    )(c, w, b.reshape(1, n))


def _mod_rows(ada_ref, s):
    return (ada_ref[0, 3 * s:3 * s + 1, :], ada_ref[0, 3 * s + 1:3 * s + 2, :],
            ada_ref[0, 3 * s + 2:3 * s + 3, :])


def _ffn_kernel(h_ref, ada_ref, gains_ref, wg_ref, wu_ref, wd_ref, o_ref, *, s, resid):
    h = h_ref[...]
    shift, scale, gate = _mod_rows(ada_ref, s)
    u = (_rms(h) * gains_ref[2 * s:2 * s + 1, :] * (1.0 + scale) + shift).astype(BF16)
    acc = jnp.zeros(h.shape, F32)
    for c0 in range(0, D_FF, FFN_FC):
        g = _dot(u, wg_ref[:, c0:c0 + FFN_FC])
        up = _dot(u, wu_ref[:, c0:c0 + FFN_FC])
        hid = (g * _sigmoid(g) * up).astype(BF16)
        acc = acc + _dot(hid, wd_ref[c0:c0 + FFN_FC, :])
    post = _rms(acc) * gains_ref[2 * s + 1:2 * s + 2, :]
    o_ref[...] = h + resid * gate * post


def _ffn(h, ada3, gains, wg, wu, wd, s, resid, seq):
    t, d = h.shape
    tm = FFN_TM
    per_b = seq // tm
    const = lambda i: (0, 0)
    return pl.pallas_call(
        functools.partial(_ffn_kernel, s=s, resid=resid),
        grid=(t // tm,),
        in_specs=[pl.BlockSpec((tm, d), lambda i: (i, 0)),
                  pl.BlockSpec((1, 9, d), lambda i: (i // per_b, 0, 0)),
                  pl.BlockSpec(gains.shape, const),
                  pl.BlockSpec(wg.shape, const, pipeline_mode=pl.Buffered(1)),
                  pl.BlockSpec(wu.shape, const, pipeline_mode=pl.Buffered(1)),
                  pl.BlockSpec(wd.shape, const, pipeline_mode=pl.Buffered(1))],
        out_specs=pl.BlockSpec((tm, d), lambda i: (i, 0)),
        out_shape=jax.ShapeDtypeStruct((t, d), F32),
        compiler_params=_params(("parallel",)),
        name=f"ffn{s}",
    )(h, ada3, gains, wg, wu, wd)


_QK_W = 2 * MLSTM_WIDTH
_QN_W = NSA_WIDTH
_KC_W = NSA_KV_WIDTH
_VC_W = NSA_KV_WIDTH
_KK_W = 3 * LANES
_GT_W = 2 * D_MODEL
_SM_W = LANES
_VO_T = 2 * MLSTM_WIDTH
_VN_T = 4 * LANES
_VT_W = _VO_T + _VN_T
_IF_COLS = 2 * MLSTM_HEADS
_GN_COLS = 3 * NSA_HEADS
_IN_WIDTHS = (_QK_W, _QN_W, _KC_W, _VC_W, _KK_W, _GT_W, _SM_W)
_IN_DTYPES = (BF16, BF16, BF16, BF16, BF16, BF16, F32)


def _inproj_kernel(h_ref, ada_ref, gains_ref, w_ref, wt_ref, qk_ref, qn_ref, kc_ref, vc_ref, kk_ref,
                   gt_ref, sm_ref, vt_ref, *, per_b):
    tm = h_ref.shape[0]
    h = h_ref[...]
    shift, scale, _ = _mod_rows(ada_ref, 1)
    u = (_rms(h) * gains_ref[2:3, :] * (1.0 + scale) + shift).astype(BF16)
    step = 512
    off = 0
    for ref in (qk_ref, qn_ref, kc_ref, vc_ref, kk_ref, gt_ref, sm_ref):
        width = ref.shape[1]
        for c0 in range(0, width, step):
            c1 = min(c0 + step, width)
            res = _dot(u, w_ref[:, off + c0:off + c1])
            if ref is kk_ref and c0 == 0:
                pos = (pl.program_id(0) % per_b) * tm + lax.broadcasted_iota(jnp.int32, (tm, c1 - c0), 0)
                lane = lax.broadcasted_iota(jnp.int32, (tm, c1 - c0), 1)
                hot = (lane == NSA_HEAD_DIM + pos // SEL_BLOCK) | (lane == LANES + pos // SEL_BLOCK)
                res = res + jnp.where(hot, 1.0, 0.0)
            ref[:, c0:c1] = res.astype(ref.dtype)
        off += width
    vt_ref[0:_VO_T, :] = _dot_nt(wt_ref[0:_VO_T, :], u).astype(vt_ref.dtype)
    ones_row = lax.broadcasted_iota(jnp.int32, (_VN_T, tm), 0) % LANES == NSA_HEAD_DIM
    vt_ref[_VO_T:, :] = (_dot_nt(wt_ref[_VO_T:, :], u) + jnp.where(ones_row, 1.0, 0.0)).astype(vt_ref.dtype)


def _inproj(h, ada3, gains, w, wt, seq):
    t, d = h.shape
    tm = FFN_TM
    per_b = seq // tm
    const = lambda i: (0, 0)
    return pl.pallas_call(
        functools.partial(_inproj_kernel, per_b=per_b),
        grid=(t // tm,),
        in_specs=[pl.BlockSpec((tm, d), lambda i: (i, 0)),
                  pl.BlockSpec((1, 9, d), lambda i: (i // per_b, 0, 0)),
                  pl.BlockSpec(gains.shape, const),
                  pl.BlockSpec(w.shape, const, pipeline_mode=pl.Buffered(1)),
                  pl.BlockSpec(wt.shape, const, pipeline_mode=pl.Buffered(1))],
        out_specs=[pl.BlockSpec((tm, wd), lambda i: (i, 0)) for wd in _IN_WIDTHS]
        + [pl.BlockSpec((_VT_W, tm), lambda i: (0, i))],
        out_shape=[jax.ShapeDtypeStruct((t, wd), dt) for wd, dt in zip(_IN_WIDTHS, _IN_DTYPES)]
        + [jax.ShapeDtypeStruct((_VT_W, t), BF16)],
        compiler_params=_params(("parallel",)),
        name="inproj",
    )(h, ada3, gains, w, wt)


_MX_ROWS = MLSTM_HEAD_DIM + 16


def _split3(x):
    hi = x.astype(BF16)
    r1 = x - hi.astype(F32)
    mid = r1.astype(BF16)
    return hi, mid, (r1 - mid.astype(F32)).astype(BF16)


def _mlstm_kernel(qk_ref, vot_ref, sm_ref, shift_ref, cw_ref, cb_ref, gb_ref, hg_ref, o_ref, tail, cext, mscr):
    L = qk_ref.shape[0]
    W = MLSTM_WIDTH
    dh = MLSTM_HEAD_DIM
    H = MLSTM_HEADS
    TAIL = tail.shape[0]

    @pl.when(pl.program_id(1) == 0)
    def _():
        tail[...] = jnp.zeros(tail.shape, tail.dtype)
        cext[...] = jnp.zeros(cext.shape, F32)
        mscr[...] = jnp.zeros(mscr.shape, F32)

    x = qk_ref[...]
    prev = tail[...]
    conv = cb_ref[...] + cw_ref[CONV_WIDTH - 1:CONV_WIDTH, :] * x.astype(F32)
    head = jnp.zeros((TAIL, 2 * W), F32)
    for j in range(CONV_WIDTH - 1):
        conv = conv + cw_ref[j:j + 1, :] * _dot(shift_ref[j], x)
        head = head + cw_ref[j:j + 1, :] * _dot(shift_ref[CONV_WIDTH - 1 + j, 0:TAIL, 0:TAIL], prev)
    conv = jnp.concatenate([conv[0:TAIL, :] + head, conv[TAIL:, :]], axis=0)
    tail[...] = x[L - TAIL:L, :]
    qk = conv * _sigmoid(conv)

    gts = sm_ref[...] + gb_ref[...]
    logf = jnp.minimum(gts, 0.0) - jnp.log1p(jnp.exp(-jnp.abs(gts)))
    s_idx = lax.broadcasted_iota(jnp.int32, (L, L), 0)
    t_idx = lax.broadcasted_iota(jnp.int32, (L, L), 1)
    causal = s_idx <= t_idx
    tri = jnp.where(causal, 1.0, 0.0).astype(BF16)
    logf_t = logf.T
    b_cols = sum(_dot_tn(tri, part) for part in _split3(logf))
    b_rows = sum(_dot(part, tri) for part in _split3(logf_t))
    g_rows = gts.T
    ones_tile = (lax.broadcasted_iota(jnp.int32, (_MX_ROWS - dh, L), 0) == 0).astype(BF16)

    for h in range(H):
        q = qk[:, h * dh:(h + 1) * dh].astype(BF16)
        k = (qk[:, W + h * dh:W + (h + 1) * dh] * (dh ** -0.5)).astype(BF16)
        vext = jnp.concatenate([vot_ref[h * dh:(h + 1) * dh, :], ones_tile], axis=0)
        b_row = b_rows[H + h:H + h + 1, :]
        i_row = g_rows[h:h + 1, :]
        r_col = gts[:, h:h + 1] - b_cols[:, H + h:H + h + 1]
        m_prev = mscr[h:h + 1, 0:1]
        c_old = cext[h]

        ld = jnp.where(causal, r_col + b_row, NEG_BIG)
        a = b_row + m_prev
        m_t = jnp.maximum(a, jnp.max(ld, axis=0, keepdims=True))
        p = (_dot_nt(k, q) * jnp.exp(ld - m_t)).astype(BF16)
        w_inter = jnp.exp(a - m_t)
        nd = _dot(vext, p) + w_inter * _dot_nt(c_old.astype(BF16), q)
        hh = nd[0:dh, :] * (1.0 / jnp.maximum(jnp.abs(nd[dh:dh + 1, :]), jnp.exp(-m_t)))

        b_last = b_row[:, L - 1:L]
        g_row = b_last - b_row + i_row
        m_new = jnp.maximum(b_last + m_prev, jnp.max(g_row, axis=1, keepdims=True))
        vw = (vext.astype(F32) * jnp.exp(g_row - m_new)).astype(BF16)
        cext[h] = jnp.exp(b_last + m_prev - m_new) * c_old + _dot(vw, k)
        mscr[h:h + 1, :] = jnp.broadcast_to(m_new, (1, LANES))

        hn = hh * lax.rsqrt(jnp.mean(hh * hh, axis=0, keepdims=True) + NORM_EPS) * hg_ref[h * dh:(h + 1) * dh, :]
        o_pre = vot_ref[W + h * dh:W + (h + 1) * dh, :].astype(F32)
        o_ref[h * dh:(h + 1) * dh, :] = (_sigmoid(o_pre) * hn).astype(o_ref.dtype)


def _mlstm(qk, vt, sm, shifts, conv_w, conv_b, gate_b_row, head_gain_col, bsz, seq):
    t = qk.shape[0]
    L = min(MLSTM_L, seq)
    nc = seq // L
    W = MLSTM_WIDTH
    const = lambda b, c: (0, 0)
    rows = lambda b, c: (b * nc + c, 0)
    cols = lambda b, c: (0, b * nc + c)
    return pl.pallas_call(
        _mlstm_kernel,
        grid=(bsz, nc),
        in_specs=[pl.BlockSpec((L, 2 * W), rows),
                  pl.BlockSpec((_VO_T, L), cols),
                  pl.BlockSpec((L, LANES), rows),
                  pl.BlockSpec(shifts.shape, lambda b, c: (0, 0, 0)),
                  pl.BlockSpec(conv_w.shape, const),
                  pl.BlockSpec(conv_b.shape, const),
                  pl.BlockSpec(gate_b_row.shape, const),
                  pl.BlockSpec(head_gain_col.shape, const)],
        out_specs=pl.BlockSpec((W, L), cols),
        out_shape=jax.ShapeDtypeStruct((W, t), BF16),
        scratch_shapes=[pltpu.VMEM((16, 2 * W), BF16),
                        pltpu.VMEM((MLSTM_HEADS, _MX_ROWS, MLSTM_HEAD_DIM), F32),
                        pltpu.VMEM((8, LANES), F32)],
        compiler_params=_params(("parallel", "arbitrary")),
        name="mlstm",
    )(qk, vt, sm, shifts, conv_w, conv_b, gate_b_row, head_gain_col)


def _shift_matrices(L, tail):
    t_idx = jnp.arange(L)[:, None]
    s_idx = jnp.arange(L)[None, :]
    mats = [(s_idx == t_idx - (CONV_WIDTH - 1 - j)) for j in range(CONV_WIDTH - 1)]
    mats += [(s_idx - tail == t_idx - (CONV_WIDTH - 1 - j)) & (s_idx < tail) & (t_idx < tail)
             for j in range(CONV_WIDTH - 1)]
    return jnp.stack(mats).astype(BF16)


def _compress_kernel(kc_ref, vc_ref, wa_ref, wb_ref, pea_ref, peb_ref, b1_ref, w2k_ref, b2k_ref, w2vt_ref, b2vt_ref,
                     ko_ref, vto_ref):
    def hidden(i, x):
        wa, wb = wa_ref[i], wb_ref[i]
        bias = _dot(pea_ref[i], wa) + _dot(peb_ref[i], wb) + b1_ref[i]
        pre = _dot(x, wa) + pltpu.roll(_dot(x, wb), x.shape[0] - 1, 0) + bias[0:1, :]
        return (pre * _sigmoid(pre)).astype(BF16)

    ko_ref[0] = (_dot(hidden(0, kc_ref[0]), w2k_ref[...]) + b2k_ref[...]).astype(ko_ref.dtype)
    vto_ref[0] = (_dot_nt(w2vt_ref[...], hidden(1, vc_ref[0])) + b2vt_ref[...]).astype(vto_ref.dtype)


def _compress(kc, vc, wa, wb, pea, peb, b1, w2k, b2k, w2vt, b2vt):
    bsz, n, width = kc.shape
    full = lambda a: pl.BlockSpec(a.shape, lambda b: (0,) * a.ndim)
    per_b = pl.BlockSpec((1, n, width), lambda b: (b, 0, 0))
    return pl.pallas_call(
        _compress_kernel,
        grid=(bsz,),
        in_specs=[per_b, per_b] + [full(a) for a in (wa, wb, pea, peb, b1, w2k, b2k, w2vt, b2vt)],
        out_specs=[pl.BlockSpec((1, n, NSA_KV_WIDTH), lambda b: (b, 0, 0)),
                   pl.BlockSpec((1, NSA_KV_WIDTH, n), lambda b: (b, 0, 0))],
        out_shape=[jax.ShapeDtypeStruct((bsz, n, NSA_KV_WIDTH), BF16),
                   jax.ShapeDtypeStruct((bsz, NSA_KV_WIDTH, n), BF16)],
        compiler_params=_params(("parallel",)),
        name="compress",
    )(kc, vc, wa, wb, pea, peb, b1, w2k, b2k, w2vt, b2vt)


def _softmax_cols(s):
    m = jnp.maximum(jnp.max(s, axis=0, keepdims=True), M_INIT)
    e = jnp.exp(s - m)
    return e * (1.0 / jnp.maximum(jnp.sum(e, axis=0, keepdims=True), 1e-30))


def _nsa_kernel(qn_ref, sm_ref, kc_ref, vct_ref, ks0_ref, ks1_ref, kw_ref, vs0_ref, vs1_ref, vw0_ref, vw1_ref,
                scl_ref, eye_ref, o_ref, imp_scr, cnt_scr, m_scr, den_scr, num_scr):
    TQ = qn_ref.shape[0]
    S = kw_ref.shape[0]
    NC = kc_ref.shape[1]
    TK = min(NSA_TK, S)
    NSEL = S // SEL_BLOCK
    NB = NSEL // SUBLANES
    HG = NSA_HPG
    DH = NSA_HEAD_DIM
    R = HG * TQ
    WL = min(WINDOW + TQ, S)
    q0 = pl.program_id(1) * TQ

    qt_all = _dot_nt(scl_ref[...], qn_ref[...]).astype(BF16)
    gates_t = _sigmoid(sm_ref[...]).T
    t_lane = q0 + lax.broadcasted_iota(jnp.int32, (1, TQ), 1)
    zeros_half = jnp.zeros((DH, R), BF16)
    sub = lax.broadcasted_iota(jnp.int32, (SUBLANES, TQ), 0)

    def lanes(f):
        return jnp.concatenate([f(h) for h in range(HG)], axis=1)

    def per_head(x):
        return jnp.concatenate([x] * HG, axis=1)

    def weights(s):
        m = jnp.maximum(jnp.max(s, axis=0, keepdims=True), M_INIT)
        return m, jnp.exp((s - m).astype(BF16))

    def value_sum(v_ref, k0, n, e):
        nd = _dot(v_ref[0:DH + 16, pl.ds(k0, n)], e.astype(BF16))
        return nd[0:DH, :], nd[DH:DH + 1, :]

    G = range(NSA_KV_GROUPS)
    qts = [lanes(lambda h: qt_all[(g * HG + h) * DH:(g * HG + h + 1) * DH, :]) for g in G]

    def with_half(g, other):
        return jnp.concatenate([qts[g], other] if g == 0 else [other, qts[g]], axis=0)

    def gate(g, br):
        return lanes(lambda h: gates_t[_IF_COLS + (g * HG + h) * 3 + br:
                                       _IF_COLS + (g * HG + h) * 3 + br + 1, :])

    qt_plain = [with_half(g, zeros_half) for g in G]
    w0 = pl.multiple_of(jnp.maximum(q0 + TQ - WL, 0), LANES)
    s_cmp = [_dot(kc_ref[0], qt_plain[g]) for g in G]
    s_win = [_dot(kw_ref[pl.ds(w0, WL), :], qt_plain[g]) for g in G]
    cmp_end = lax.broadcasted_iota(jnp.int32, (NC, TQ), 0) * CMP_STRIDE + (CMP_BLOCK - 1)
    bias_c = per_head(jnp.where(cmp_end <= t_lane, 0.0, NEG_BIG))
    wpos = w0 + lax.broadcasted_iota(jnp.int32, (WL, TQ), 0)
    bias_w = per_head(jnp.where((wpos <= t_lane) & (wpos > t_lane - WINDOW), 0.0, NEG_BIG))
    jj = lax.broadcasted_iota(jnp.int32, (NSEL, NC), 0) * SEL_BLOCK
    nn = lax.broadcasted_iota(jnp.int32, (NSEL, NC), 1) * CMP_STRIDE
    overlap = jnp.where((nn < jj + SEL_BLOCK) & (nn + CMP_BLOCK > jj), 1.0, 0.0).astype(BF16)
    blk = lax.broadcasted_iota(jnp.int32, (NSEL, TQ), 0)
    cur = t_lane // SEL_BLOCK
    forced = (blk == 0) | (blk == cur) | (blk == cur - 1)

    o_rest, imps = [], []
    for g in G:
        p = _softmax_cols(s_cmp[g] + bias_c)
        o_cmp = _dot(vct_ref[0], p.astype(BF16))[g * DH:(g + 1) * DH, :]
        psum = p[:, 0:TQ]
        for h in range(1, HG):
            psum = psum + p[:, h * TQ:(h + 1) * TQ]
        imp = sum(_dot(overlap, part) for part in _split3(psum))
        imp = jnp.where(blk * SEL_BLOCK <= t_lane, jnp.where(forced, FORCE_SCORE, imp), -1.0)
        imp_scr[g] = imp
        imps.append(imp)
        _, e = weights(s_win[g] + bias_w)
        num, den = value_sum((vw0_ref, vw1_ref)[g], w0, WL, e)
        o_rest.append(gate(g, 0) * o_cmp + gate(g, 2) * (num * (1.0 / jnp.maximum(den, 1e-30))))

    cnt_scr[...] = jnp.zeros(cnt_scr.shape, F32)
    for ib in range(NB):
        @pl.when(ib * SUBLANES * SEL_BLOCK < q0 + TQ)
        def _():
            for g in G:
                cnt = [cnt_scr[g, jb * SUBLANES:(jb + 1) * SUBLANES, :] for jb in range(NB)]
                for u in range(SUBLANES):
                    r = imp_scr[g, ib * SUBLANES + u:ib * SUBLANES + u + 1, :]
                    for jb in range(NB):
                        x = imps[g][jb * SUBLANES:(jb + 1) * SUBLANES, :]
                        if jb > ib:
                            ahead = jnp.where(r >= x, 1.0, 0.0)
                        elif jb < ib:
                            ahead = jnp.where(r > x, 1.0, 0.0)
                        else:
                            ahead = jnp.where(r > x, 1.0, jnp.where((r == x) & (sub > u), 1.0, 0.0))
                        cnt[jb] = cnt[jb] + ahead
                for jb in range(NB):
                    cnt_scr[g, jb * SUBLANES:(jb + 1) * SUBLANES, :] = cnt[jb]

    groups = []
    for g in G:
        selb = jnp.where(cnt_scr[g] < float(min(N_SELECT, NSEL)), 0.0, NEG_BIG)
        if NSEL < DH:
            selb = jnp.concatenate([selb, jnp.zeros((DH - NSEL, TQ), F32)], axis=0)
        groups.append((with_half(g, per_head(selb.astype(BF16))), o_rest[g], gate(g, 1)))

    def sel_scores(g, k0, diagonal):
        s = _dot((ks0_ref, ks1_ref)[g][pl.ds(k0, TK), :], groups[g][0])
        if diagonal:
            kpos = k0 + lax.broadcasted_iota(jnp.int32, (TK, TQ), 0)
            s = s + per_head(jnp.where(kpos <= t_lane, 0.0, NEG_BIG))
        return s

    def sel_fold(g, k0, s):
        m_t, e = weights(s)
        num_t, den_t = value_sum((vs0_ref, vs1_ref)[g], k0, TK, e)
        m = m_scr[g]
        m_new = jnp.maximum(m, m_t)
        a, b = jnp.exp(m - m_new), jnp.exp(m_t - m_new)
        m_scr[g] = m_new
        den_scr[g] = a * den_scr[g] + b * den_t
        num_scr[g] = a * num_scr[g] + b * num_t

    def sel_block(tiles):
        scores = [[sel_scores(g, k0, diagonal) for g in range(NSA_KV_GROUPS)] for k0, diagonal in tiles]
        for (k0, _), per_group in zip(tiles, scores):
            for g, s in enumerate(per_group):
                sel_fold(g, k0, s)

    m_scr[...] = jnp.full(m_scr.shape, M_INIT, F32)
    den_scr[...] = jnp.zeros(den_scr.shape, F32)
    num_scr[...] = jnp.zeros(num_scr.shape, F32)
    n_full = q0 // TK

    def pair_body(j, c):
        k0 = pl.multiple_of(j * (2 * TK), 2 * TK)
        sel_block([(k0, False), (k0 + TK, False)])
        return c

    lax.fori_loop(0, n_full // 2, pair_body, 0)
    k_diag = pl.multiple_of(n_full * TK, TK)

    @pl.when(n_full % 2 == 1)
    def _():
        sel_block([(k_diag - TK, False), (k_diag, True)])

    @pl.when(n_full % 2 == 0)
    def _():
        sel_block([(k_diag, True)])

    out_rows = []
    for g, (_, o_rest, gate_sel) in enumerate(groups):
        o = o_rest + gate_sel * (num_scr[g] * (1.0 / jnp.maximum(den_scr[g], 1e-30)))
        out_rows += [o[:, h * TQ:(h + 1) * TQ] for h in range(HG)]

    o_ref[...] = _dot_tn(jnp.concatenate(out_rows, axis=0).astype(BF16), eye_ref[...]).astype(o_ref.dtype)


def _nsa(qn, sm, k_cmp, v_cmp_t, kk, vt, scl, eye, bsz, seq):
    t = qn.shape[0]
    tq = min(NSA_TQ, seq)
    nq = seq // tq
    nc = k_cmp.shape[1]
    nsel = seq // SEL_BLOCK
    assert nsel <= NSA_HEAD_DIM and nsel % SUBLANES == 0 and seq % min(NSA_TK, seq) == 0
    rows = lambda b, i: (b * nq + i, 0)
    const = lambda b, i: (0, 0)
    key_piece = lambda j: pl.BlockSpec((seq, LANES), lambda b, i: (b, j))
    val_piece = lambda j: pl.BlockSpec((LANES, seq), lambda b, i: (_VO_T // LANES + j, b))
    return pl.pallas_call(
        _nsa_kernel,
        grid=(bsz, nq),
        in_specs=[pl.BlockSpec((tq, NSA_WIDTH), rows),
                  pl.BlockSpec((tq, LANES), rows),
                  pl.BlockSpec((1, nc, NSA_KV_WIDTH), lambda b, i: (b, 0, 0)),
                  pl.BlockSpec((1, NSA_KV_WIDTH, nc), lambda b, i: (b, 0, 0)),
                  key_piece(0), key_piece(1), key_piece(2),
                  val_piece(0), val_piece(1), val_piece(2), val_piece(3),
                  pl.BlockSpec(scl.shape, const), pl.BlockSpec(eye.shape, const)],
        out_specs=pl.BlockSpec((tq, NSA_WIDTH), rows),
        out_shape=jax.ShapeDtypeStruct((t, NSA_WIDTH), BF16),
        scratch_shapes=[pltpu.VMEM((NSA_KV_GROUPS, nsel, tq), F32), pltpu.VMEM((NSA_KV_GROUPS, nsel, tq), F32),
                        pltpu.VMEM((NSA_KV_GROUPS, 1, NSA_HPG * tq), F32),
                        pltpu.VMEM((NSA_KV_GROUPS, 1, NSA_HPG * tq), F32),
                        pltpu.VMEM((NSA_KV_GROUPS, NSA_HEAD_DIM, NSA_HPG * tq), F32)],
        compiler_params=_params(("parallel", "arbitrary")),
        name="nsa",
    )(qn, sm, k_cmp, v_cmp_t, kk, kk, kk, vt, vt, vt, vt, scl, eye)


def _merge_kernel(h_ref, ada_ref, gains_ref, hm_ref, hn_ref, gt_ref, wm_ref, wn_ref, wo_ref, o_ref):
    d = h_ref.shape[1]
    h = h_ref[...]
    _, _, gate = _mod_rows(ada_ref, 1)
    y_m = _dot_tn(hm_ref[...], wm_ref[...])
    y_n = _dot(hn_ref[...], wn_ref[...])
    merged = (_sigmoid(gt_ref[:, :d].astype(F32)) * y_m + _sigmoid(gt_ref[:, d:].astype(F32)) * y_n)
    z = _dot(merged.astype(BF16), wo_ref[...])
    o_ref[...] = h + gate * (_rms(z) * gains_ref[3:4, :])


def _merge(h, ada3, gains, hm, hn, gt, wm, wn, wo, seq):
    t, d = h.shape
    tm = FFN_TM
    per_b = seq // tm
    const = lambda i: (0, 0)
    rows = lambda i: (i, 0)
    return pl.pallas_call(
        _merge_kernel,
        grid=(t // tm,),
        in_specs=[pl.BlockSpec((tm, d), rows),
                  pl.BlockSpec((1, 9, d), lambda i: (i // per_b, 0, 0)),
                  pl.BlockSpec(gains.shape, const),
                  pl.BlockSpec((hm.shape[0], tm), lambda i: (0, i)),
                  pl.BlockSpec((tm, hn.shape[1]), rows),
                  pl.BlockSpec((tm, gt.shape[1]), rows),
                  pl.BlockSpec(wm.shape, const), pl.BlockSpec(wn.shape, const), pl.BlockSpec(wo.shape, const)],
        out_specs=pl.BlockSpec((tm, d), rows),
        out_shape=jax.ShapeDtypeStruct((t, d), F32),
        compiler_params=_params(("parallel",)),
        name="merge",
    )(h, ada3, gains, hm, hn, gt, wm, wn, wo)


def _reorder_w_in(w_in):
    m, kvw, dh = MLSTM_WIDTH, NSA_KV_WIDTH, NSA_HEAD_DIM
    o_if = 4 * m
    o_qn = o_if + _IF_COLS
    o_kc = o_qn + NSA_WIDTH
    o_vc, o_ks, o_vs, o_kw, o_vw = (o_kc + i * kvw for i in range(1, 6))
    o_gn = o_vw + kvw
    o_gm = o_gn + _GN_COLS
    zeros = lambda n: jnp.zeros((w_in.shape[0], n), w_in.dtype)
    w = jnp.concatenate([
        w_in[:, :2 * m], w_in[:, o_qn:o_kc], w_in[:, o_kc:o_vc], w_in[:, o_vc:o_ks],
        w_in[:, o_ks:o_ks + dh], zeros(dh), zeros(dh), w_in[:, o_ks + dh:o_vs], w_in[:, o_kw:o_vw],
        w_in[:, o_gm:],
        w_in[:, o_if:o_qn], w_in[:, o_gn:o_gm], zeros(_SM_W - _IF_COLS - _GN_COLS)], axis=1)
    wt = jnp.concatenate([w_in[:, 2 * m:o_if]]
                         + [piece for o in (o_vs, o_vw) for g in range(NSA_KV_GROUPS)
                            for piece in (w_in[:, o + g * dh:o + (g + 1) * dh], zeros(LANES - dh))], axis=1).T
    return w.astype(BF16), wt.astype(BF16)


def _compress_weights(cmp_pe, cmp_w1, cmp_b1, cmp_w2, cmp_b2):
    G, dh, hid = NSA_KV_GROUPS, NSA_HEAD_DIM, CMP_HIDDEN
    eye = jnp.eye(G, dtype=F32)
    w1 = cmp_w1.reshape(2, CMP_BLOCK, dh, hid)
    w1 = jnp.einsum('kpdj,gh->kpgdhj', w1, eye).reshape(2, CMP_BLOCK * G * dh, G * hid)
    half = CMP_STRIDE * G * dh
    wa, wb = w1[:, :half].astype(BF16), w1[:, half:].astype(BF16)
    pe = jnp.broadcast_to(cmp_pe[:, :, None, :], (2, CMP_BLOCK, G, dh)).reshape(2, 1, CMP_BLOCK * G * dh)
    pe = jnp.broadcast_to(pe, (2, 8, CMP_BLOCK * G * dh)).astype(BF16)
    pea, peb = pe[:, :, :half], pe[:, :, half:]
    b1 = jnp.tile(cmp_b1, (1, G)).reshape(2, 1, G * hid)
    w2 = jnp.einsum('kjd,gh->kgjhd', cmp_w2, eye).reshape(2, G * hid, G * dh).astype(BF16)
    b2 = jnp.tile(cmp_b2, (1, G))
    return wa, wb, pea, peb, b1, w2[0], b2[0].reshape(1, G * dh), w2[1].T, b2[1].reshape(G * dh, 1)


def kernel(x, c, w_ada, b_ada, norm_gains, ffn_wg, ffn_wu, ffn_wd, w_in, mlstm_conv_w, mlstm_conv_b,
           mlstm_gate_b, mlstm_head_gain, cmp_pe, cmp_w1, cmp_b1, cmp_w2, cmp_b2, w_up_mlstm, w_up_nsa, w_out):
    bsz, seq, d = x.shape
    t = bsz * seq
    h = x.reshape(t, d)
    eye = jnp.eye(NSA_WIDTH, dtype=BF16)
    scl = (jnp.eye(NSA_WIDTH, dtype=F32) * NSA_HEAD_DIM ** -0.5).astype(BF16)
    for l in range(w_ada.shape[0]):
        gains = norm_gains[l]
        ada3 = _ada(c, w_ada[l], b_ada[l]).reshape(bsz, 9, d)
        wg, wu, wd = ffn_wg[l].astype(BF16), ffn_wu[l].astype(BF16), ffn_wd[l].astype(BF16)

        h = _ffn(h, ada3, gains, wg[0], wu[0], wd[0], 0, 0.5, seq)

        qk, qn, kc, vc, kk, gt, sm, vt = _inproj(h, ada3, gains, *_reorder_w_in(w_in[l]), seq)
        gate_b_row = jnp.concatenate([mlstm_gate_b[l], jnp.zeros((LANES - _IF_COLS,), F32)]).reshape(1, LANES)
        hm = _mlstm(qk, vt, sm, _shift_matrices(min(MLSTM_L, seq), 16),
                    mlstm_conv_w[l].reshape(CONV_WIDTH, 2 * MLSTM_WIDTH), mlstm_conv_b[l].reshape(1, -1),
                    gate_b_row, mlstm_head_gain[l].reshape(-1, 1), bsz, seq)

        chunked = (bsz, seq // CMP_STRIDE, CMP_STRIDE * NSA_KV_WIDTH)
        k_cmp, v_cmp_t = _compress(kc.reshape(chunked), vc.reshape(chunked),
                                   *_compress_weights(cmp_pe[l], cmp_w1[l], cmp_b1[l], cmp_w2[l], cmp_b2[l]))
        hn = _nsa(qn, sm, k_cmp, v_cmp_t, kk, vt, scl, eye, bsz, seq)

        h = _merge(h, ada3, gains, hm, hn, gt, w_up_mlstm[l].astype(BF16), w_up_nsa[l].astype(BF16),
                   w_out[l].astype(BF16), seq)
        h = _ffn(h, ada3, gains, wg[1], wu[1], wd[1], 2, 0.5, seq)
    return h.reshape(bsz, seq, d)
```

```python
import functools

import jax
import jax.numpy as jnp
from jax import lax
from jax.experimental import pallas as pl
from jax.experimental.pallas import tpu as pltpu

D_MODEL = 1024
D_FF = 2816
NORM_EPS = 1e-6
CONV_WIDTH = 4
MLSTM_HEADS = 4
MLSTM_HEAD_DIM = 128
MLSTM_WIDTH = MLSTM_HEADS * MLSTM_HEAD_DIM
NSA_HEADS = 8
NSA_KV_GROUPS = 2
NSA_HPG = NSA_HEADS // NSA_KV_GROUPS
NSA_HEAD_DIM = 64
NSA_WIDTH = NSA_HEADS * NSA_HEAD_DIM
NSA_KV_WIDTH = NSA_KV_GROUPS * NSA_HEAD_DIM
CMP_BLOCK = 32
CMP_STRIDE = 16
CMP_HIDDEN = 256
SEL_BLOCK = 64
N_SELECT = 16
WINDOW = 256
FORCE_SCORE = 1e4
NEG_BIG = -1e30
M_INIT = -1e29

LANES = 128
SUBLANES = 8
VMEM_LIMIT = 56 * 1024 * 1024

FFN_TM = 512
FFN_SUB = 512
FFN_ROWS = 1024
FFN_FC = 256
MLSTM_L = 256
NSA_TQ = 256
NSA_TK = 256
SEL_UNROLL = 4

BF16 = jnp.bfloat16
F32 = jnp.float32


def _params(sem):
    return pltpu.CompilerParams(dimension_semantics=sem, vmem_limit_bytes=VMEM_LIMIT)


def _rms(y):
    return y * lax.rsqrt(jnp.mean(y * y, axis=-1, keepdims=True) + NORM_EPS)


def _sigmoid(x):
    return 0.5 * jnp.tanh(0.5 * x) + 0.5


def _dot(a, b):
    return jnp.dot(a, b, preferred_element_type=F32)


def _dot_nt(a, b, precision=None):
    return lax.dot_general(a, b, (((1,), (1,)), ((), ())), preferred_element_type=F32, precision=precision)


def _dot_tn(a, b):
    return lax.dot_general(a, b, (((0,), (0,)), ((), ())), preferred_element_type=F32)


def _ada_kernel(c_ref, w_ref, b_ref, o_ref):
    c = c_ref[...]
    sc = c * _sigmoid(c)
    o_ref[...] = jnp.dot(sc, w_ref[...], preferred_element_type=F32,
                         precision=lax.Precision.HIGHEST) + b_ref[...]


def _ada(c, w, b):
    bsz, d = c.shape
    n = w.shape[1]
    tn = 1152
    return pl.pallas_call(
        _ada_kernel,
        grid=(n // tn,),
        in_specs=[pl.BlockSpec((bsz, d), lambda j: (0, 0)),
                  pl.BlockSpec((d, tn), lambda j: (0, j)),
                  pl.BlockSpec((1, tn), lambda j: (0, j))],
        out_specs=pl.BlockSpec((bsz, tn), lambda j: (0, j)),
        out_shape=jax.ShapeDtypeStruct((bsz, n), F32),
        compiler_params=_params(("arbitrary",)),
        name="ada",
    )(c, w, b.reshape(1, n))


def _mod_rows(ada_ref, s):
    return (ada_ref[0, 3 * s:3 * s + 1, :], ada_ref[0, 3 * s + 1:3 * s + 2, :],
            ada_ref[0, 3 * s + 2:3 * s + 3, :])


def _ffn_kernel(h_ref, ada_ref, gains_ref, wg_ref, wu_ref, wd_ref, o_ref, *, s, resid):
    shift, scale, gate = _mod_rows(ada_ref, s)
    subs = [slice(r0, r0 + FFN_SUB) for r0 in range(0, h_ref.shape[0], FFN_SUB)]
    us = [(_rms(h_ref[rows, :]) * gains_ref[2 * s:2 * s + 1, :] * (1.0 + scale) + shift).astype(BF16)
          for rows in subs]
    accs = [jnp.zeros((FFN_SUB, h_ref.shape[1]), F32) for _ in subs]
    for c0 in range(0, D_FF, FFN_FC):
        for i, u in enumerate(us):
            g = _dot(u, wg_ref[:, c0:c0 + FFN_FC])
            up = _dot(u, wu_ref[:, c0:c0 + FFN_FC])
            hid = (g * _sigmoid(g) * up).astype(BF16)
            accs[i] = accs[i] + _dot(hid, wd_ref[c0:c0 + FFN_FC, :])
    for rows, acc in zip(subs, accs):
        post = _rms(acc) * gains_ref[2 * s + 1:2 * s + 2, :]
        o_ref[rows, :] = h_ref[rows, :] + resid * gate * post


def _ffn(h, ada3, gains, wg, wu, wd, s, resid, seq):
    t, d = h.shape
    tm = FFN_ROWS
    per_b = seq // tm
    const = lambda i: (0, 0)
    return pl.pallas_call(
        functools.partial(_ffn_kernel, s=s, resid=resid),
        grid=(t // tm,),
        in_specs=[pl.BlockSpec((tm, d), lambda i: (i, 0)),
                  pl.BlockSpec((1, 9, d), lambda i: (i // per_b, 0, 0)),
                  pl.BlockSpec(gains.shape, const),
                  pl.BlockSpec(wg.shape, const, pipeline_mode=pl.Buffered(1)),
                  pl.BlockSpec(wu.shape, const, pipeline_mode=pl.Buffered(1)),
                  pl.BlockSpec(wd.shape, const, pipeline_mode=pl.Buffered(1))],
        out_specs=pl.BlockSpec((tm, d), lambda i: (i, 0)),
        out_shape=jax.ShapeDtypeStruct((t, d), F32),
        compiler_params=_params(("parallel",)),
        name=f"ffn{s}",
    )(h, ada3, gains, wg, wu, wd)


_QK_W = 2 * MLSTM_WIDTH
_QN_W = NSA_WIDTH
_KC_W = NSA_KV_WIDTH
_VC_W = NSA_KV_WIDTH
_KK_W = 3 * LANES
_GT_W = 2 * D_MODEL
_SM_W = LANES
_VO_T = 2 * MLSTM_WIDTH
_VN_T = 4 * LANES
_VT_W = _VO_T + _VN_T
_IF_COLS = 2 * MLSTM_HEADS
_GN_COLS = 3 * NSA_HEADS
_IN_WIDTHS = (_QK_W, _QN_W, _KC_W, _VC_W, _KK_W, _GT_W, _SM_W)
_IN_DTYPES = (BF16, BF16, BF16, BF16, BF16, BF16, F32)


def _inproj_kernel(h_ref, ada_ref, gains_ref, w_ref, wt_ref, qk_ref, qn_ref, kc_ref, vc_ref, kk_ref,
                   gt_ref, sm_ref, vt_ref, *, per_b):
    tm = h_ref.shape[0]
    h = h_ref[...]
    shift, scale, _ = _mod_rows(ada_ref, 1)
    u = (_rms(h) * gains_ref[2:3, :] * (1.0 + scale) + shift).astype(BF16)
    step = 512
    off = 0
    for ref in (qk_ref, qn_ref, kc_ref, vc_ref, kk_ref, gt_ref, sm_ref):
        width = ref.shape[1]
        for c0 in range(0, width, step):
            c1 = min(c0 + step, width)
            res = _dot(u, w_ref[:, off + c0:off + c1])
            if ref is kk_ref and c0 == 0:
                pos = (pl.program_id(0) % per_b) * tm + lax.broadcasted_iota(jnp.int32, (tm, c1 - c0), 0)
                lane = lax.broadcasted_iota(jnp.int32, (tm, c1 - c0), 1)
                hot = (lane == NSA_HEAD_DIM + pos // SEL_BLOCK) | (lane == LANES + pos // SEL_BLOCK)
                res = res + jnp.where(hot, 1.0, 0.0)
            ref[:, c0:c1] = res.astype(ref.dtype)
        off += width
    vt_ref[0:_VO_T, :] = _dot_nt(wt_ref[0:_VO_T, :], u).astype(vt_ref.dtype)
    ones_row = lax.broadcasted_iota(jnp.int32, (_VN_T, tm), 0) % LANES == NSA_HEAD_DIM
    vt_ref[_VO_T:, :] = (_dot_nt(wt_ref[_VO_T:, :], u) + jnp.where(ones_row, 1.0, 0.0)).astype(vt_ref.dtype)


def _inproj(h, ada3, gains, w, wt, seq):
    t, d = h.shape
    tm = FFN_TM
    per_b = seq // tm
    const = lambda i: (0, 0)
    return pl.pallas_call(
        functools.partial(_inproj_kernel, per_b=per_b),
        grid=(t // tm,),
        in_specs=[pl.BlockSpec((tm, d), lambda i: (i, 0)),
                  pl.BlockSpec((1, 9, d), lambda i: (i // per_b, 0, 0)),
                  pl.BlockSpec(gains.shape, const),
                  pl.BlockSpec(w.shape, const, pipeline_mode=pl.Buffered(1)),
                  pl.BlockSpec(wt.shape, const, pipeline_mode=pl.Buffered(1))],
        out_specs=[pl.BlockSpec((tm, wd), lambda i: (i, 0)) for wd in _IN_WIDTHS]
        + [pl.BlockSpec((_VT_W, tm), lambda i: (0, i))],
        out_shape=[jax.ShapeDtypeStruct((t, wd), dt) for wd, dt in zip(_IN_WIDTHS, _IN_DTYPES)]
        + [jax.ShapeDtypeStruct((_VT_W, t), BF16)],
        compiler_params=_params(("parallel",)),
        name="inproj",
    )(h, ada3, gains, w, wt)


_MX_ROWS = MLSTM_HEAD_DIM + 16


def _split3(x):
    hi = x.astype(BF16)
    r1 = x - hi.astype(F32)
    mid = r1.astype(BF16)
    return hi, mid, (r1 - mid.astype(F32)).astype(BF16)


def _mlstm_kernel(qk_ref, vot_ref, sm_ref, shift_ref, cw_ref, cb_ref, gb_ref, hg_ref, o_ref, tail, cext, mscr):
    L = qk_ref.shape[0]
    W = MLSTM_WIDTH
    dh = MLSTM_HEAD_DIM
    H = MLSTM_HEADS
    TAIL = tail.shape[0]

    @pl.when(pl.program_id(1) == 0)
    def _():
        tail[...] = jnp.zeros(tail.shape, tail.dtype)
        cext[...] = jnp.zeros(cext.shape, F32)
        mscr[...] = jnp.zeros(mscr.shape, F32)

    x = qk_ref[...]
    prev = tail[...]
    conv = cb_ref[...] + cw_ref[CONV_WIDTH - 1:CONV_WIDTH, :] * x.astype(F32)
    head = jnp.zeros((TAIL, 2 * W), F32)
    for j in range(CONV_WIDTH - 1):
        conv = conv + cw_ref[j:j + 1, :] * _dot(shift_ref[j], x)
        head = head + cw_ref[j:j + 1, :] * _dot(shift_ref[CONV_WIDTH - 1 + j, 0:TAIL, 0:TAIL], prev)
    conv = jnp.concatenate([conv[0:TAIL, :] + head, conv[TAIL:, :]], axis=0)
    tail[...] = x[L - TAIL:L, :]
    qk = conv * _sigmoid(conv)

    gts = sm_ref[...] + gb_ref[...]
    logf = jnp.minimum(gts, 0.0) - jnp.log1p(jnp.exp(-jnp.abs(gts)))
    s_idx = lax.broadcasted_iota(jnp.int32, (L, L), 0)
    t_idx = lax.broadcasted_iota(jnp.int32, (L, L), 1)
    causal = s_idx <= t_idx
    tri = jnp.where(causal, 1.0, 0.0).astype(BF16)
    logf_t = logf.T
    b_cols = sum(_dot_tn(tri, part) for part in _split3(logf))
    b_rows = sum(_dot(part, tri) for part in _split3(logf_t))
    g_rows = gts.T
    ones_tile = (lax.broadcasted_iota(jnp.int32, (_MX_ROWS - dh, L), 0) == 0).astype(BF16)

    for h in range(H):
        q = qk[:, h * dh:(h + 1) * dh].astype(BF16)
        k = (qk[:, W + h * dh:W + (h + 1) * dh] * (dh ** -0.5)).astype(BF16)
        vext = jnp.concatenate([vot_ref[h * dh:(h + 1) * dh, :], ones_tile], axis=0)
        b_row = b_rows[H + h:H + h + 1, :]
        i_row = g_rows[h:h + 1, :]
        r_col = gts[:, h:h + 1] - b_cols[:, H + h:H + h + 1]
        m_prev = mscr[h:h + 1, 0:1]
        c_old = cext[h]

        ld = jnp.where(causal, r_col + b_row, NEG_BIG)
        a = b_row + m_prev
        m_t = jnp.maximum(a, jnp.max(ld, axis=0, keepdims=True))
        p = (_dot_nt(k, q) * jnp.exp(ld - m_t)).astype(BF16)
        w_inter = jnp.exp(a - m_t)
        nd = _dot(vext, p) + w_inter * _dot_nt(c_old.astype(BF16), q)
        hh = nd[0:dh, :] * (1.0 / jnp.maximum(jnp.abs(nd[dh:dh + 1, :]), jnp.exp(-m_t)))

        b_last = b_row[:, L - 1:L]
        g_row = b_last - b_row + i_row
        m_new = jnp.maximum(b_last + m_prev, jnp.max(g_row, axis=1, keepdims=True))
        vw = (vext.astype(F32) * jnp.exp(g_row - m_new)).astype(BF16)
        cext[h] = jnp.exp(b_last + m_prev - m_new) * c_old + _dot(vw, k)
        mscr[h:h + 1, :] = jnp.broadcast_to(m_new, (1, LANES))

        hn = hh * lax.rsqrt(jnp.mean(hh * hh, axis=0, keepdims=True) + NORM_EPS) * hg_ref[h * dh:(h + 1) * dh, :]
        o_pre = vot_ref[W + h * dh:W + (h + 1) * dh, :].astype(F32)
        o_ref[h * dh:(h + 1) * dh, :] = (_sigmoid(o_pre) * hn).astype(o_ref.dtype)


def _mlstm(qk, vt, sm, shifts, conv_w, conv_b, gate_b_row, head_gain_col, bsz, seq):
    t = qk.shape[0]
    L = min(MLSTM_L, seq)
    nc = seq // L
    W = MLSTM_WIDTH
    const = lambda b, c: (0, 0)
    rows = lambda b, c: (b * nc + c, 0)
    cols = lambda b, c: (0, b * nc + c)
    return pl.pallas_call(
        _mlstm_kernel,
        grid=(bsz, nc),
        in_specs=[pl.BlockSpec((L, 2 * W), rows),
                  pl.BlockSpec((_VO_T, L), cols),
                  pl.BlockSpec((L, LANES), rows),
                  pl.BlockSpec(shifts.shape, lambda b, c: (0, 0, 0)),
                  pl.BlockSpec(conv_w.shape, const),
                  pl.BlockSpec(conv_b.shape, const),
                  pl.BlockSpec(gate_b_row.shape, const),
                  pl.BlockSpec(head_gain_col.shape, const)],
        out_specs=pl.BlockSpec((W, L), cols),
        out_shape=jax.ShapeDtypeStruct((W, t), BF16),
        scratch_shapes=[pltpu.VMEM((16, 2 * W), BF16),
                        pltpu.VMEM((MLSTM_HEADS, _MX_ROWS, MLSTM_HEAD_DIM), F32),
                        pltpu.VMEM((8, LANES), F32)],
        compiler_params=_params(("parallel", "arbitrary")),
        name="mlstm",
    )(qk, vt, sm, shifts, conv_w, conv_b, gate_b_row, head_gain_col)


def _shift_matrices(L, tail):
    t_idx = jnp.arange(L)[:, None]
    s_idx = jnp.arange(L)[None, :]
    mats = [(s_idx == t_idx - (CONV_WIDTH - 1 - j)) for j in range(CONV_WIDTH - 1)]
    mats += [(s_idx - tail == t_idx - (CONV_WIDTH - 1 - j)) & (s_idx < tail) & (t_idx < tail)
             for j in range(CONV_WIDTH - 1)]
    return jnp.stack(mats).astype(BF16)


def _compress_kernel(kc_ref, vc_ref, wa_ref, wb_ref, pea_ref, peb_ref, b1_ref, w2k_ref, b2k_ref, w2vt_ref, b2vt_ref,
                     ko_ref, vto_ref):
    def hidden(i, x):
        wa, wb = wa_ref[i], wb_ref[i]
        bias = _dot(pea_ref[i], wa) + _dot(peb_ref[i], wb) + b1_ref[i]
        pre = _dot(x, wa) + pltpu.roll(_dot(x, wb), x.shape[0] - 1, 0) + bias[0:1, :]
        return (pre * _sigmoid(pre)).astype(BF16)

    ko_ref[0] = (_dot(hidden(0, kc_ref[0]), w2k_ref[...]) + b2k_ref[...]).astype(ko_ref.dtype)
    vto_ref[0] = (_dot_nt(w2vt_ref[...], hidden(1, vc_ref[0])) + b2vt_ref[...]).astype(vto_ref.dtype)


def _compress(kc, vc, wa, wb, pea, peb, b1, w2k, b2k, w2vt, b2vt):
    bsz, n, width = kc.shape
    full = lambda a: pl.BlockSpec(a.shape, lambda b: (0,) * a.ndim)
    per_b = pl.BlockSpec((1, n, width), lambda b: (b, 0, 0))
    return pl.pallas_call(
        _compress_kernel,
        grid=(bsz,),
        in_specs=[per_b, per_b] + [full(a) for a in (wa, wb, pea, peb, b1, w2k, b2k, w2vt, b2vt)],
        out_specs=[pl.BlockSpec((1, n, NSA_KV_WIDTH), lambda b: (b, 0, 0)),
                   pl.BlockSpec((1, NSA_KV_WIDTH, n), lambda b: (b, 0, 0))],
        out_shape=[jax.ShapeDtypeStruct((bsz, n, NSA_KV_WIDTH), BF16),
                   jax.ShapeDtypeStruct((bsz, NSA_KV_WIDTH, n), BF16)],
        compiler_params=_params(("parallel",)),
        name="compress",
    )(kc, vc, wa, wb, pea, peb, b1, w2k, b2k, w2vt, b2vt)


def _softmax_cols(s):
    m = jnp.maximum(jnp.max(s, axis=0, keepdims=True), M_INIT)
    e = jnp.exp(s - m)
    return e * (1.0 / jnp.maximum(jnp.sum(e, axis=0, keepdims=True), 1e-30))


def _nsa_kernel(qn_ref, sm_ref, kc_ref, vct_ref, ks0_ref, ks1_ref, kw_ref, vs0_ref, vs1_ref, vw0_ref, vw1_ref,
                scl_ref, eye_ref, o_ref, imp_scr, cnt_scr, m_scr, den_scr, num_scr):
    TQ = qn_ref.shape[0]
    S = kw_ref.shape[0]
    NC = kc_ref.shape[1]
    TK = min(NSA_TK, S)
    NSEL = S // SEL_BLOCK
    NB = NSEL // SUBLANES
    HG = NSA_HPG
    DH = NSA_HEAD_DIM
    R = HG * TQ
    WL = min(WINDOW + TQ, S)
    q0 = pl.program_id(1) * TQ

    qt_all = _dot_nt(scl_ref[...], qn_ref[...]).astype(BF16)
    gates_t = _sigmoid(sm_ref[...]).T
    t_lane = q0 + lax.broadcasted_iota(jnp.int32, (1, TQ), 1)
    zeros_half = jnp.zeros((DH, R), BF16)
    sub = lax.broadcasted_iota(jnp.int32, (SUBLANES, TQ), 0)

    def lanes(f):
        return jnp.concatenate([f(h) for h in range(HG)], axis=1)

    def per_head(x):
        return jnp.concatenate([x] * HG, axis=1)

    def weights(s):
        m = jnp.maximum(jnp.max(s, axis=0, keepdims=True), M_INIT)
        return m, jnp.exp((s - m).astype(BF16))

    def value_sum(v_ref, k0, n, e):
        nd = _dot(v_ref[0:DH + 16, pl.ds(k0, n)], e.astype(BF16))
        return nd[0:DH, :], nd[DH:DH + 1, :]

    G = range(NSA_KV_GROUPS)
    qts = [lanes(lambda h: qt_all[(g * HG + h) * DH:(g * HG + h + 1) * DH, :]) for g in G]

    def with_half(g, other):
        return jnp.concatenate([qts[g], other] if g == 0 else [other, qts[g]], axis=0)

    def gate(g, br):
        return lanes(lambda h: gates_t[_IF_COLS + (g * HG + h) * 3 + br:
                                       _IF_COLS + (g * HG + h) * 3 + br + 1, :])

    qt_plain = [with_half(g, zeros_half) for g in G]
    w0 = pl.multiple_of(jnp.maximum(q0 + TQ - WL, 0), LANES)
    s_cmp = [_dot(kc_ref[0], qt_plain[g]) for g in G]
    s_win = [_dot(kw_ref[pl.ds(w0, WL), :], qt_plain[g]) for g in G]
    cmp_end = lax.broadcasted_iota(jnp.int32, (NC, TQ), 0) * CMP_STRIDE + (CMP_BLOCK - 1)
    bias_c = per_head(jnp.where(cmp_end <= t_lane, 0.0, NEG_BIG))
    wpos = w0 + lax.broadcasted_iota(jnp.int32, (WL, TQ), 0)
    bias_w = per_head(jnp.where((wpos <= t_lane) & (wpos > t_lane - WINDOW), 0.0, NEG_BIG))
    jj = lax.broadcasted_iota(jnp.int32, (NSEL, NC), 0) * SEL_BLOCK
    nn = lax.broadcasted_iota(jnp.int32, (NSEL, NC), 1) * CMP_STRIDE
    overlap = jnp.where((nn < jj + SEL_BLOCK) & (nn + CMP_BLOCK > jj), 1.0, 0.0).astype(BF16)
    blk = lax.broadcasted_iota(jnp.int32, (NSEL, TQ), 0)
    cur = t_lane // SEL_BLOCK
    forced = (blk == 0) | (blk == cur) | (blk == cur - 1)

    o_rest, imps = [], []
    for g in G:
        p = _softmax_cols(s_cmp[g] + bias_c)
        o_cmp = _dot(vct_ref[0], p.astype(BF16))[g * DH:(g + 1) * DH, :]
        psum = p[:, 0:TQ]
        for h in range(1, HG):
            psum = psum + p[:, h * TQ:(h + 1) * TQ]
        imp = sum(_dot(overlap, part) for part in _split3(psum))
        imp = jnp.where(blk * SEL_BLOCK <= t_lane, jnp.where(forced, FORCE_SCORE, imp), -1.0)
        imp_scr[g] = imp
        imps.append(imp)
        _, e = weights(s_win[g] + bias_w)
        num, den = value_sum((vw0_ref, vw1_ref)[g], w0, WL, e)
        o_rest.append(gate(g, 0) * o_cmp + gate(g, 2) * (num * (1.0 / jnp.maximum(den, 1e-30))))

    cnt_scr[...] = jnp.zeros(cnt_scr.shape, F32)
    for ib in range(NB):
        @pl.when(ib * SUBLANES * SEL_BLOCK < q0 + TQ)
        def _():
            for g in G:
                cnt = [cnt_scr[g, jb * SUBLANES:(jb + 1) * SUBLANES, :] for jb in range(NB)]
                for u in range(SUBLANES):
                    r = imp_scr[g, ib * SUBLANES + u:ib * SUBLANES + u + 1, :]
                    for jb in range(NB):
                        x = imps[g][jb * SUBLANES:(jb + 1) * SUBLANES, :]
                        if jb > ib:
                            ahead = jnp.where(r >= x, 1.0, 0.0)
                        elif jb < ib:
                            ahead = jnp.where(r > x, 1.0, 0.0)
                        else:
                            ahead = jnp.where(r > x, 1.0, jnp.where((r == x) & (sub > u), 1.0, 0.0))
                        cnt[jb] = cnt[jb] + ahead
                for jb in range(NB):
                    cnt_scr[g, jb * SUBLANES:(jb + 1) * SUBLANES, :] = cnt[jb]

    groups = []
    for g in G:
        selb = jnp.where(cnt_scr[g] < float(min(N_SELECT, NSEL)), 0.0, NEG_BIG)
        if NSEL < DH:
            selb = jnp.concatenate([selb, jnp.zeros((DH - NSEL, TQ), F32)], axis=0)
        groups.append((with_half(g, per_head(selb.astype(BF16))), o_rest[g], gate(g, 1)))

    def sel_scores(g, k0, diagonal):
        s = _dot((ks0_ref, ks1_ref)[g][pl.ds(k0, TK), :], groups[g][0])
        if diagonal:
            kpos = k0 + lax.broadcasted_iota(jnp.int32, (TK, TQ), 0)
            s = s + per_head(jnp.where(kpos <= t_lane, 0.0, NEG_BIG))
        return s

    def sel_fold(g, k0, s):
        m_t, e = weights(s)
        num_t, den_t = value_sum((vs0_ref, vs1_ref)[g], k0, TK, e)
        m = m_scr[g]
        m_new = jnp.maximum(m, m_t)
        a, b = jnp.exp(m - m_new), jnp.exp(m_t - m_new)
        m_scr[g] = m_new
        den_scr[g] = a * den_scr[g] + b * den_t
        num_scr[g] = a * num_scr[g] + b * num_t

    def sel_block(tiles):
        scores = [[sel_scores(g, k0, diagonal) for g in G] for k0, diagonal in tiles]
        for (k0, _), per_group in zip(tiles, scores):
            for g, s in enumerate(per_group):
                sel_fold(g, k0, s)

    m_scr[...] = jnp.full(m_scr.shape, M_INIT, F32)
    den_scr[...] = jnp.zeros(den_scr.shape, F32)
    num_scr[...] = jnp.zeros(num_scr.shape, F32)
    n_full = q0 // TK

    def block_body(j, c):
        k0 = pl.multiple_of(j * (SEL_UNROLL * TK), SEL_UNROLL * TK)
        sel_block([(k0 + i * TK, False) for i in range(SEL_UNROLL)])
        return c

    lax.fori_loop(0, n_full // SEL_UNROLL, block_body, 0)
    k_diag = pl.multiple_of(n_full * TK, TK)

    for rem in range(SEL_UNROLL):
        @pl.when(n_full % SEL_UNROLL == rem)
        def _():
            sel_block([(pl.multiple_of(k_diag - (rem - i) * TK, TK), i == rem) for i in range(rem + 1)])

    out_rows = []
    for g, (_, o_rest, gate_sel) in enumerate(groups):
        o = o_rest + gate_sel * (num_scr[g] * (1.0 / jnp.maximum(den_scr[g], 1e-30)))
        out_rows += [o[:, h * TQ:(h + 1) * TQ] for h in range(HG)]

    o_ref[...] = _dot_tn(jnp.concatenate(out_rows, axis=0).astype(BF16), eye_ref[...]).astype(o_ref.dtype)


def _nsa(qn, sm, k_cmp, v_cmp_t, kk, vt, scl, eye, bsz, seq):
    t = qn.shape[0]
    tq = min(NSA_TQ, seq)
    nq = seq // tq
    nc = k_cmp.shape[1]
    nsel = seq // SEL_BLOCK
    assert nsel <= NSA_HEAD_DIM and nsel % SUBLANES == 0 and seq % min(NSA_TK, seq) == 0
    rows = lambda b, i: (b * nq + i, 0)
    const = lambda b, i: (0, 0)
    key_piece = lambda j: pl.BlockSpec((seq, LANES), lambda b, i: (b, j))
    val_piece = lambda j: pl.BlockSpec((LANES, seq), lambda b, i: (_VO_T // LANES + j, b))
    return pl.pallas_call(
        _nsa_kernel,
        grid=(bsz, nq),
        in_specs=[pl.BlockSpec((tq, NSA_WIDTH), rows),
                  pl.BlockSpec((tq, LANES), rows),
                  pl.BlockSpec((1, nc, NSA_KV_WIDTH), lambda b, i: (b, 0, 0)),
                  pl.BlockSpec((1, NSA_KV_WIDTH, nc), lambda b, i: (b, 0, 0)),
                  key_piece(0), key_piece(1), key_piece(2),
                  val_piece(0), val_piece(1), val_piece(2), val_piece(3),
                  pl.BlockSpec(scl.shape, const), pl.BlockSpec(eye.shape, const)],
        out_specs=pl.BlockSpec((tq, NSA_WIDTH), rows),
        out_shape=jax.ShapeDtypeStruct((t, NSA_WIDTH), BF16),
        scratch_shapes=[pltpu.VMEM((NSA_KV_GROUPS, nsel, tq), F32), pltpu.VMEM((NSA_KV_GROUPS, nsel, tq), F32),
                        pltpu.VMEM((NSA_KV_GROUPS, 1, NSA_HPG * tq), F32),
                        pltpu.VMEM((NSA_KV_GROUPS, 1, NSA_HPG * tq), F32),
                        pltpu.VMEM((NSA_KV_GROUPS, NSA_HEAD_DIM, NSA_HPG * tq), F32)],
        compiler_params=_params(("parallel", "arbitrary")),
        name="nsa",
    )(qn, sm, k_cmp, v_cmp_t, kk, kk, kk, vt, vt, vt, vt, scl, eye)


def _merge_kernel(h_ref, ada_ref, gains_ref, hm_ref, hn_ref, gt_ref, wm_ref, wn_ref, wo_ref, o_ref):
    d = h_ref.shape[1]
    h = h_ref[...]
    _, _, gate = _mod_rows(ada_ref, 1)
    y_m = _dot_tn(hm_ref[...], wm_ref[...])
    y_n = _dot(hn_ref[...], wn_ref[...])
    merged = (_sigmoid(gt_ref[:, :d].astype(F32)) * y_m + _sigmoid(gt_ref[:, d:].astype(F32)) * y_n)
    z = _dot(merged.astype(BF16), wo_ref[...])
    o_ref[...] = h + gate * (_rms(z) * gains_ref[3:4, :])


def _merge(h, ada3, gains, hm, hn, gt, wm, wn, wo, seq):
    t, d = h.shape
    tm = FFN_TM
    per_b = seq // tm
    const = lambda i: (0, 0)
    rows = lambda i: (i, 0)
    return pl.pallas_call(
        _merge_kernel,
        grid=(t // tm,),
        in_specs=[pl.BlockSpec((tm, d), rows),
                  pl.BlockSpec((1, 9, d), lambda i: (i // per_b, 0, 0)),
                  pl.BlockSpec(gains.shape, const),
                  pl.BlockSpec((hm.shape[0], tm), lambda i: (0, i)),
                  pl.BlockSpec((tm, hn.shape[1]), rows),
                  pl.BlockSpec((tm, gt.shape[1]), rows),
                  pl.BlockSpec(wm.shape, const), pl.BlockSpec(wn.shape, const), pl.BlockSpec(wo.shape, const)],
        out_specs=pl.BlockSpec((tm, d), rows),
        out_shape=jax.ShapeDtypeStruct((t, d), F32),
        compiler_params=_params(("parallel",)),
        name="merge",
    )(h, ada3, gains, hm, hn, gt, wm, wn, wo)


def _reorder_w_in(w_in):
    m, kvw, dh = MLSTM_WIDTH, NSA_KV_WIDTH, NSA_HEAD_DIM
    o_if = 4 * m
    o_qn = o_if + _IF_COLS
    o_kc = o_qn + NSA_WIDTH
    o_vc, o_ks, o_vs, o_kw, o_vw = (o_kc + i * kvw for i in range(1, 6))
    o_gn = o_vw + kvw
    o_gm = o_gn + _GN_COLS
    zeros = lambda n: jnp.zeros((w_in.shape[0], n), w_in.dtype)
    w = jnp.concatenate([
        w_in[:, :2 * m], w_in[:, o_qn:o_kc], w_in[:, o_kc:o_vc], w_in[:, o_vc:o_ks],
        w_in[:, o_ks:o_ks + dh], zeros(dh), zeros(dh), w_in[:, o_ks + dh:o_vs], w_in[:, o_kw:o_vw],
        w_in[:, o_gm:],
        w_in[:, o_if:o_qn], w_in[:, o_gn:o_gm], zeros(_SM_W - _IF_COLS - _GN_COLS)], axis=1)
    wt = jnp.concatenate([w_in[:, 2 * m:o_if]]
                         + [piece for o in (o_vs, o_vw) for g in range(NSA_KV_GROUPS)
                            for piece in (w_in[:, o + g * dh:o + (g + 1) * dh], zeros(LANES - dh))], axis=1).T
    return w.astype(BF16), wt.astype(BF16)


def _compress_weights(cmp_pe, cmp_w1, cmp_b1, cmp_w2, cmp_b2):
    G, dh, hid = NSA_KV_GROUPS, NSA_HEAD_DIM, CMP_HIDDEN
    eye = jnp.eye(G, dtype=F32)
    w1 = cmp_w1.reshape(2, CMP_BLOCK, dh, hid)
    w1 = jnp.einsum('kpdj,gh->kpgdhj', w1, eye).reshape(2, CMP_BLOCK * G * dh, G * hid)
    half = CMP_STRIDE * G * dh
    wa, wb = w1[:, :half].astype(BF16), w1[:, half:].astype(BF16)
    pe = jnp.broadcast_to(cmp_pe[:, :, None, :], (2, CMP_BLOCK, G, dh)).reshape(2, 1, CMP_BLOCK * G * dh)
    pe = jnp.broadcast_to(pe, (2, 8, CMP_BLOCK * G * dh)).astype(BF16)
    pea, peb = pe[:, :, :half], pe[:, :, half:]
    b1 = jnp.tile(cmp_b1, (1, G)).reshape(2, 1, G * hid)
    w2 = jnp.einsum('kjd,gh->kgjhd', cmp_w2, eye).reshape(2, G * hid, G * dh).astype(BF16)
    b2 = jnp.tile(cmp_b2, (1, G))
    return wa, wb, pea, peb, b1, w2[0], b2[0].reshape(1, G * dh), w2[1].T, b2[1].reshape(G * dh, 1)


def kernel(x, c, w_ada, b_ada, norm_gains, ffn_wg, ffn_wu, ffn_wd, w_in, mlstm_conv_w, mlstm_conv_b,
           mlstm_gate_b, mlstm_head_gain, cmp_pe, cmp_w1, cmp_b1, cmp_w2, cmp_b2, w_up_mlstm, w_up_nsa, w_out):
    bsz, seq, d = x.shape
    t = bsz * seq
    h = x.reshape(t, d)
    eye = jnp.eye(NSA_WIDTH, dtype=BF16)
    scl = (jnp.eye(NSA_WIDTH, dtype=F32) * NSA_HEAD_DIM ** -0.5).astype(BF16)
    for l in range(w_ada.shape[0]):
        gains = norm_gains[l]
        ada3 = _ada(c, w_ada[l], b_ada[l]).reshape(bsz, 9, d)
        wg, wu, wd = ffn_wg[l].astype(BF16), ffn_wu[l].astype(BF16), ffn_wd[l].astype(BF16)

        h = _ffn(h, ada3, gains, wg[0], wu[0], wd[0], 0, 0.5, seq)

        qk, qn, kc, vc, kk, gt, sm, vt = _inproj(h, ada3, gains, *_reorder_w_in(w_in[l]), seq)
        gate_b_row = jnp.concatenate([mlstm_gate_b[l], jnp.zeros((LANES - _IF_COLS,), F32)]).reshape(1, LANES)
        hm = _mlstm(qk, vt, sm, _shift_matrices(min(MLSTM_L, seq), 16),
                    mlstm_conv_w[l].reshape(CONV_WIDTH, 2 * MLSTM_WIDTH), mlstm_conv_b[l].reshape(1, -1),
                    gate_b_row, mlstm_head_gain[l].reshape(-1, 1), bsz, seq)

        chunked = (bsz, seq // CMP_STRIDE, CMP_STRIDE * NSA_KV_WIDTH)
        k_cmp, v_cmp_t = _compress(kc.reshape(chunked), vc.reshape(chunked),
                                   *_compress_weights(cmp_pe[l], cmp_w1[l], cmp_b1[l], cmp_w2[l], cmp_b2[l]))
        hn = _nsa(qn, sm, k_cmp, v_cmp_t, kk, vt, scl, eye, bsz, seq)

        h = _merge(h, ada3, gains, hm, hn, gt, w_up_mlstm[l].astype(BF16), w_up_nsa[l].astype(BF16),
                   w_out[l].astype(BF16), seq)
        h = _ffn(h, ada3, gains, wg[1], wu[1], wd[1], 2, 0.5, seq)
    return h.reshape(bsz, seq, d)
```

```python
import functools

import jax
import jax.numpy as jnp
from jax import lax
from jax.experimental import pallas as pl
from jax.experimental.pallas import tpu as pltpu

D_MODEL = 1024
D_FF = 2816
NORM_EPS = 1e-6
CONV_WIDTH = 4
MLSTM_HEADS = 4
MLSTM_HEAD_DIM = 128
MLSTM_WIDTH = MLSTM_HEADS * MLSTM_HEAD_DIM
NSA_HEADS = 8
NSA_KV_GROUPS = 2
NSA_HPG = NSA_HEADS // NSA_KV_GROUPS
NSA_HEAD_DIM = 64
NSA_WIDTH = NSA_HEADS * NSA_HEAD_DIM
NSA_KV_WIDTH = NSA_KV_GROUPS * NSA_HEAD_DIM
CMP_BLOCK = 32
CMP_STRIDE = 16
CMP_HIDDEN = 256
SEL_BLOCK = 64
N_SELECT = 16
WINDOW = 256
FORCE_SCORE = 1e4
NEG_BIG = -1e30
M_INIT = -1e29

LANES = 128
SUBLANES = 8
VMEM_LIMIT = 56 * 1024 * 1024

FFN_SUB = 512
FFN_ROWS = 1024
FFN_FC = 256
MLSTM_L = 256
NSA_TQ = 256
NSA_TK = 256
SEL_UNROLL = 4

BF16 = jnp.bfloat16
F32 = jnp.float32


def _params(sem):
    return pltpu.CompilerParams(dimension_semantics=sem, vmem_limit_bytes=VMEM_LIMIT)


def _rms(y):
    return y * lax.rsqrt(jnp.mean(y * y, axis=-1, keepdims=True) + NORM_EPS)


def _sigmoid(x):
    return 0.5 * jnp.tanh(0.5 * x) + 0.5


def _dot(a, b):
    return jnp.dot(a, b, preferred_element_type=F32)


def _dot_nt(a, b, precision=None):
    return lax.dot_general(a, b, (((1,), (1,)), ((), ())), preferred_element_type=F32, precision=precision)


def _dot_tn(a, b):
    return lax.dot_general(a, b, (((0,), (0,)), ((), ())), preferred_element_type=F32)


def _ada_kernel(c_ref, w_ref, b_ref, o_ref):
    c = c_ref[...]
    sc = c * _sigmoid(c)
    o_ref[...] = jnp.dot(sc, w_ref[...], preferred_element_type=F32,
                         precision=lax.Precision.HIGHEST) + b_ref[...]


def _ada(c, w, b):
    bsz, d = c.shape
    n = w.shape[1]
    tn = 1152
    return pl.pallas_call(
        _ada_kernel,
        grid=(n // tn,),
        in_specs=[pl.BlockSpec((bsz, d), lambda j: (0, 0)),
                  pl.BlockSpec((d, tn), lambda j: (0, j)),
                  pl.BlockSpec((1, tn), lambda j: (0, j))],
        out_specs=pl.BlockSpec((bsz, tn), lambda j: (0, j)),
        out_shape=jax.ShapeDtypeStruct((bsz, n), F32),
        compiler_params=_params(("arbitrary",)),
        name="ada",
    )(c, w, b.reshape(1, n))


def _mod_rows(ada_ref, s):
    return (ada_ref[0, 3 * s:3 * s + 1, :], ada_ref[0, 3 * s + 1:3 * s + 2, :],
            ada_ref[0, 3 * s + 2:3 * s + 3, :])


def _ffn_kernel(h_ref, ada_ref, gains_ref, wg_ref, wu_ref, wd_ref, o_ref, *, s, resid):
    shift, scale, gate = _mod_rows(ada_ref, s)
    subs = [slice(r0, r0 + FFN_SUB) for r0 in range(0, h_ref.shape[0], FFN_SUB)]
    us = [(_rms(h_ref[rows, :]) * gains_ref[2 * s:2 * s + 1, :] * (1.0 + scale) + shift).astype(BF16)
          for rows in subs]
    accs = [jnp.zeros((FFN_SUB, h_ref.shape[1]), F32) for _ in subs]
    for c0 in range(0, D_FF, FFN_FC):
        for i, u in enumerate(us):
            g = _dot(u, wg_ref[:, c0:c0 + FFN_FC])
            up = _dot(u, wu_ref[:, c0:c0 + FFN_FC])
            hid = (g * _sigmoid(g) * up).astype(BF16)
            accs[i] = accs[i] + _dot(hid, wd_ref[c0:c0 + FFN_FC, :])
    for rows, acc in zip(subs, accs):
        post = _rms(acc) * gains_ref[2 * s + 1:2 * s + 2, :]
        o_ref[rows, :] = h_ref[rows, :] + resid * gate * post


def _ffn(h, ada3, gains, wg, wu, wd, s, resid, seq):
    t, d = h.shape
    tm = FFN_ROWS
    per_b = seq // tm
    const = lambda i: (0, 0)
    return pl.pallas_call(
        functools.partial(_ffn_kernel, s=s, resid=resid),
        grid=(t // tm,),
        in_specs=[pl.BlockSpec((tm, d), lambda i: (i, 0)),
                  pl.BlockSpec((1, 9, d), lambda i: (i // per_b, 0, 0)),
                  pl.BlockSpec(gains.shape, const),
                  pl.BlockSpec(wg.shape, const, pipeline_mode=pl.Buffered(1)),
                  pl.BlockSpec(wu.shape, const, pipeline_mode=pl.Buffered(1)),
                  pl.BlockSpec(wd.shape, const, pipeline_mode=pl.Buffered(1))],
        out_specs=pl.BlockSpec((tm, d), lambda i: (i, 0)),
        out_shape=jax.ShapeDtypeStruct((t, d), F32),
        compiler_params=_params(("parallel",)),
        name=f"ffn{s}",
    )(h, ada3, gains, wg, wu, wd)


_QK_W = 2 * MLSTM_WIDTH
_QN_W = NSA_WIDTH
_KC_W = NSA_KV_WIDTH
_VC_W = NSA_KV_WIDTH
_KK_W = 3 * LANES
_GT_W = 2 * D_MODEL
_SM_W = LANES
_VO_T = 2 * MLSTM_WIDTH
_VN_T = 4 * LANES
_VT_W = _VO_T + _VN_T
_IF_COLS = 2 * MLSTM_HEADS
_GN_COLS = 3 * NSA_HEADS
_IN_WIDTHS = (_QK_W, _QN_W, _KC_W, _VC_W, _KK_W, _GT_W, _SM_W)
_IN_DTYPES = (BF16, BF16, BF16, BF16, BF16, BF16, F32)


def _inproj_kernel(h_ref, ada_ref, gains_ref, w_ref, wt_ref, qk_ref, qn_ref, kc_ref, vc_ref, kk_ref,
                   gt_ref, sm_ref, vt_ref, *, per_b):
    tm = h_ref.shape[0]
    shift, scale, _ = _mod_rows(ada_ref, 1)
    subs = [slice(r0, r0 + FFN_SUB) for r0 in range(0, tm, FFN_SUB)]
    us = [(_rms(h_ref[rows, :]) * gains_ref[2:3, :] * (1.0 + scale) + shift).astype(BF16) for rows in subs]
    step = 512
    off = 0
    for ref in (qk_ref, qn_ref, kc_ref, vc_ref, kk_ref, gt_ref, sm_ref):
        width = ref.shape[1]
        for c0 in range(0, width, step):
            c1 = min(c0 + step, width)
            for rows, u in zip(subs, us):
                res = _dot(u, w_ref[:, off + c0:off + c1])
                if ref is kk_ref and c0 == 0:
                    pos = ((pl.program_id(0) % per_b) * tm + rows.start
                           + lax.broadcasted_iota(jnp.int32, (FFN_SUB, c1 - c0), 0))
                    lane = lax.broadcasted_iota(jnp.int32, (FFN_SUB, c1 - c0), 1)
                    hot = (lane == NSA_HEAD_DIM + pos // SEL_BLOCK) | (lane == LANES + pos // SEL_BLOCK)
                    res = res + jnp.where(hot, 1.0, 0.0)
                ref[rows, c0:c1] = res.astype(ref.dtype)
        off += width
    ones_row = lax.broadcasted_iota(jnp.int32, (_VN_T, FFN_SUB), 0) % LANES == NSA_HEAD_DIM
    for rows, u in zip(subs, us):
        vt_ref[0:_VO_T, rows] = _dot_nt(wt_ref[0:_VO_T, :], u).astype(vt_ref.dtype)
        vt_ref[_VO_T:, rows] = (_dot_nt(wt_ref[_VO_T:, :], u) + jnp.where(ones_row, 1.0, 0.0)).astype(vt_ref.dtype)


def _inproj(h, ada3, gains, w, wt, seq):
    t, d = h.shape
    tm = FFN_ROWS
    per_b = seq // tm
    const = lambda i: (0, 0)
    return pl.pallas_call(
        functools.partial(_inproj_kernel, per_b=per_b),
        grid=(t // tm,),
        in_specs=[pl.BlockSpec((tm, d), lambda i: (i, 0)),
                  pl.BlockSpec((1, 9, d), lambda i: (i // per_b, 0, 0)),
                  pl.BlockSpec(gains.shape, const),
                  pl.BlockSpec(w.shape, const, pipeline_mode=pl.Buffered(1)),
                  pl.BlockSpec(wt.shape, const, pipeline_mode=pl.Buffered(1))],
        out_specs=[pl.BlockSpec((tm, wd), lambda i: (i, 0)) for wd in _IN_WIDTHS]
        + [pl.BlockSpec((_VT_W, tm), lambda i: (0, i))],
        out_shape=[jax.ShapeDtypeStruct((t, wd), dt) for wd, dt in zip(_IN_WIDTHS, _IN_DTYPES)]
        + [jax.ShapeDtypeStruct((_VT_W, t), BF16)],
        compiler_params=_params(("parallel",)),
        name="inproj",
    )(h, ada3, gains, w, wt)


_MX_ROWS = MLSTM_HEAD_DIM + 16


def _split3(x):
    hi = x.astype(BF16)
    r1 = x - hi.astype(F32)
    mid = r1.astype(BF16)
    return hi, mid, (r1 - mid.astype(F32)).astype(BF16)


def _mlstm_kernel(qk_ref, vot_ref, sm_ref, shift_ref, cw_ref, cb_ref, gb_ref, hg_ref, o_ref, tail, cext, mscr):
    L = qk_ref.shape[0]
    W = MLSTM_WIDTH
    dh = MLSTM_HEAD_DIM
    H = MLSTM_HEADS
    TAIL = tail.shape[0]

    @pl.when(pl.program_id(1) == 0)
    def _():
        tail[...] = jnp.zeros(tail.shape, tail.dtype)
        cext[...] = jnp.zeros(cext.shape, F32)
        mscr[...] = jnp.zeros(mscr.shape, F32)

    x = qk_ref[...]
    prev = tail[...]
    conv = cb_ref[...] + cw_ref[CONV_WIDTH - 1:CONV_WIDTH, :] * x.astype(F32)
    head = jnp.zeros((TAIL, 2 * W), F32)
    for j in range(CONV_WIDTH - 1):
        conv = conv + cw_ref[j:j + 1, :] * _dot(shift_ref[j], x)
        head = head + cw_ref[j:j + 1, :] * _dot(shift_ref[CONV_WIDTH - 1 + j, 0:TAIL, 0:TAIL], prev)
    conv = jnp.concatenate([conv[0:TAIL, :] + head, conv[TAIL:, :]], axis=0)
    tail[...] = x[L - TAIL:L, :]
    qk = conv * _sigmoid(conv)

    gts = sm_ref[...] + gb_ref[...]
    logf = jnp.minimum(gts, 0.0) - jnp.log1p(jnp.exp(-jnp.abs(gts)))
    s_idx = lax.broadcasted_iota(jnp.int32, (L, L), 0)
    t_idx = lax.broadcasted_iota(jnp.int32, (L, L), 1)
    causal = s_idx <= t_idx
    tri = jnp.where(causal, 1.0, 0.0).astype(BF16)
    logf_t = logf.T
    b_cols = sum(_dot_tn(tri, part) for part in _split3(logf))
    b_rows = sum(_dot(part, tri) for part in _split3(logf_t))
    g_rows = gts.T
    ones_tile = (lax.broadcasted_iota(jnp.int32, (_MX_ROWS - dh, L), 0) == 0).astype(BF16)

    for h in range(H):
        q = qk[:, h * dh:(h + 1) * dh].astype(BF16)
        k = (qk[:, W + h * dh:W + (h + 1) * dh] * (dh ** -0.5)).astype(BF16)
        vext = jnp.concatenate([vot_ref[h * dh:(h + 1) * dh, :], ones_tile], axis=0)
        b_row = b_rows[H + h:H + h + 1, :]
        i_row = g_rows[h:h + 1, :]
        r_col = gts[:, h:h + 1] - b_cols[:, H + h:H + h + 1]
        m_prev = mscr[h:h + 1, 0:1]
        c_old = cext[h]

        ld = jnp.where(causal, r_col + b_row, NEG_BIG)
        a = b_row + m_prev
        m_t = jnp.maximum(a, jnp.max(ld, axis=0, keepdims=True))
        p = (_dot_nt(k, q) * jnp.exp(ld - m_t)).astype(BF16)
        w_inter = jnp.exp(a - m_t)
        nd = _dot(vext, p) + w_inter * _dot_nt(c_old.astype(BF16), q)
        hh = nd[0:dh, :] * (1.0 / jnp.maximum(jnp.abs(nd[dh:dh + 1, :]), jnp.exp(-m_t)))

        b_last = b_row[:, L - 1:L]
        g_row = b_last - b_row + i_row
        m_new = jnp.maximum(b_last + m_prev, jnp.max(g_row, axis=1, keepdims=True))
        vw = (vext.astype(F32) * jnp.exp(g_row - m_new)).astype(BF16)
        cext[h] = jnp.exp(b_last + m_prev - m_new) * c_old + _dot(vw, k)
        mscr[h:h + 1, :] = jnp.broadcast_to(m_new, (1, LANES))

        hn = hh * lax.rsqrt(jnp.mean(hh * hh, axis=0, keepdims=True) + NORM_EPS) * hg_ref[h * dh:(h + 1) * dh, :]
        o_pre = vot_ref[W + h * dh:W + (h + 1) * dh, :].astype(F32)
        o_ref[h * dh:(h + 1) * dh, :] = (_sigmoid(o_pre) * hn).astype(o_ref.dtype)


def _mlstm(qk, vt, sm, shifts, conv_w, conv_b, gate_b_row, head_gain_col, bsz, seq):
    t = qk.shape[0]
    L = min(MLSTM_L, seq)
    nc = seq // L
    W = MLSTM_WIDTH
    const = lambda b, c: (0, 0)
    rows = lambda b, c: (b * nc + c, 0)
    cols = lambda b, c: (0, b * nc + c)
    return pl.pallas_call(
        _mlstm_kernel,
        grid=(bsz, nc),
        in_specs=[pl.BlockSpec((L, 2 * W), rows),
                  pl.BlockSpec((_VO_T, L), cols),
                  pl.BlockSpec((L, LANES), rows),
                  pl.BlockSpec(shifts.shape, lambda b, c: (0, 0, 0)),
                  pl.BlockSpec(conv_w.shape, const),
                  pl.BlockSpec(conv_b.shape, const),
                  pl.BlockSpec(gate_b_row.shape, const),
                  pl.BlockSpec(head_gain_col.shape, const)],
        out_specs=pl.BlockSpec((W, L), cols),
        out_shape=jax.ShapeDtypeStruct((W, t), BF16),
        scratch_shapes=[pltpu.VMEM((16, 2 * W), BF16),
                        pltpu.VMEM((MLSTM_HEADS, _MX_ROWS, MLSTM_HEAD_DIM), F32),
                        pltpu.VMEM((8, LANES), F32)],
        compiler_params=_params(("parallel", "arbitrary")),
        name="mlstm",
    )(qk, vt, sm, shifts, conv_w, conv_b, gate_b_row, head_gain_col)


def _shift_matrices(L, tail):
    t_idx = jnp.arange(L)[:, None]
    s_idx = jnp.arange(L)[None, :]
    mats = [(s_idx == t_idx - (CONV_WIDTH - 1 - j)) for j in range(CONV_WIDTH - 1)]
    mats += [(s_idx - tail == t_idx - (CONV_WIDTH - 1 - j)) & (s_idx < tail) & (t_idx < tail)
             for j in range(CONV_WIDTH - 1)]
    return jnp.stack(mats).astype(BF16)


def _compress_kernel(kc_ref, vc_ref, wa_ref, wb_ref, pea_ref, peb_ref, b1_ref, w2k_ref, b2k_ref, w2vt_ref, b2vt_ref,
                     ko_ref, vto_ref):
    def hidden(i, x):
        wa, wb = wa_ref[i], wb_ref[i]
        bias = _dot(pea_ref[i], wa) + _dot(peb_ref[i], wb) + b1_ref[i]
        pre = _dot(x, wa) + pltpu.roll(_dot(x, wb), x.shape[0] - 1, 0) + bias[0:1, :]
        return (pre * _sigmoid(pre)).astype(BF16)

    ko_ref[0] = (_dot(hidden(0, kc_ref[0]), w2k_ref[...]) + b2k_ref[...]).astype(ko_ref.dtype)
    vto_ref[0] = (_dot_nt(w2vt_ref[...], hidden(1, vc_ref[0])) + b2vt_ref[...]).astype(vto_ref.dtype)


def _compress(kc, vc, wa, wb, pea, peb, b1, w2k, b2k, w2vt, b2vt):
    bsz, n, width = kc.shape
    full = lambda a: pl.BlockSpec(a.shape, lambda b: (0,) * a.ndim)
    per_b = pl.BlockSpec((1, n, width), lambda b: (b, 0, 0))
    return pl.pallas_call(
        _compress_kernel,
        grid=(bsz,),
        in_specs=[per_b, per_b] + [full(a) for a in (wa, wb, pea, peb, b1, w2k, b2k, w2vt, b2vt)],
        out_specs=[pl.BlockSpec((1, n, NSA_KV_WIDTH), lambda b: (b, 0, 0)),
                   pl.BlockSpec((1, NSA_KV_WIDTH, n), lambda b: (b, 0, 0))],
        out_shape=[jax.ShapeDtypeStruct((bsz, n, NSA_KV_WIDTH), BF16),
                   jax.ShapeDtypeStruct((bsz, NSA_KV_WIDTH, n), BF16)],
        compiler_params=_params(("parallel",)),
        name="compress",
    )(kc, vc, wa, wb, pea, peb, b1, w2k, b2k, w2vt, b2vt)


def _softmax_cols(s):
    m = jnp.maximum(jnp.max(s, axis=0, keepdims=True), M_INIT)
    e = jnp.exp(s - m)
    return e * (1.0 / jnp.maximum(jnp.sum(e, axis=0, keepdims=True), 1e-30))


def _nsa_kernel(qn_ref, sm_ref, kc_ref, vct_ref, ks0_ref, ks1_ref, kw_ref, vs0_ref, vs1_ref, vw0_ref, vw1_ref,
                scl_ref, eye_ref, o_ref, imp_scr, cnt_scr, m_scr, den_scr, num_scr):
    TQ = qn_ref.shape[0]
    S = kw_ref.shape[0]
    NC = kc_ref.shape[1]
    TK = min(NSA_TK, S)
    NSEL = S // SEL_BLOCK
    NB = NSEL // SUBLANES
    HG = NSA_HPG
    DH = NSA_HEAD_DIM
    R = HG * TQ
    WL = min(WINDOW + TQ, S)
    q0 = pl.program_id(1) * TQ

    qt_all = _dot_nt(scl_ref[...], qn_ref[...]).astype(BF16)
    gates_t = _sigmoid(sm_ref[...]).T
    t_lane = q0 + lax.broadcasted_iota(jnp.int32, (1, TQ), 1)
    zeros_half = jnp.zeros((DH, R), BF16)
    sub = lax.broadcasted_iota(jnp.int32, (SUBLANES, TQ), 0)

    def lanes(f):
        return jnp.concatenate([f(h) for h in range(HG)], axis=1)

    def per_head(x):
        return jnp.concatenate([x] * HG, axis=1)

    def weights(s):
        m = jnp.maximum(jnp.max(s, axis=0, keepdims=True), M_INIT)
        return m, jnp.exp((s - m).astype(BF16))

    def value_sum(v_ref, k0, n, e):
        nd = _dot(v_ref[0:DH + 16, pl.ds(k0, n)], e.astype(BF16))
        return nd[0:DH, :], nd[DH:DH + 1, :]

    G = range(NSA_KV_GROUPS)
    qts = [lanes(lambda h: qt_all[(g * HG + h) * DH:(g * HG + h + 1) * DH, :]) for g in G]

    def with_half(g, other):
        return jnp.concatenate([qts[g], other] if g == 0 else [other, qts[g]], axis=0)

    def gate(g, br):
        return lanes(lambda h: gates_t[_IF_COLS + (g * HG + h) * 3 + br:
                                       _IF_COLS + (g * HG + h) * 3 + br + 1, :])

    qt_plain = [with_half(g, zeros_half) for g in G]
    w0 = pl.multiple_of(jnp.maximum(q0 + TQ - WL, 0), LANES)
    s_cmp = [_dot(kc_ref[0], qt_plain[g]) for g in G]
    s_win = [_dot(kw_ref[pl.ds(w0, WL), :], qt_plain[g]) for g in G]
    cmp_end = lax.broadcasted_iota(jnp.int32, (NC, TQ), 0) * CMP_STRIDE + (CMP_BLOCK - 1)
    bias_c = per_head(jnp.where(cmp_end <= t_lane, 0.0, NEG_BIG))
    wpos = w0 + lax.broadcasted_iota(jnp.int32, (WL, TQ), 0)
    bias_w = per_head(jnp.where((wpos <= t_lane) & (wpos > t_lane - WINDOW), 0.0, NEG_BIG))
    jj = lax.broadcasted_iota(jnp.int32, (NSEL, NC), 0) * SEL_BLOCK
    nn = lax.broadcasted_iota(jnp.int32, (NSEL, NC), 1) * CMP_STRIDE
    overlap = jnp.where((nn < jj + SEL_BLOCK) & (nn + CMP_BLOCK > jj), 1.0, 0.0).astype(BF16)
    blk = lax.broadcasted_iota(jnp.int32, (NSEL, TQ), 0)
    cur = t_lane // SEL_BLOCK
    forced = (blk == 0) | (blk == cur) | (blk == cur - 1)

    o_rest, imps = [], []
    for g in G:
        p = _softmax_cols(s_cmp[g] + bias_c)
        o_cmp = _dot(vct_ref[0], p.astype(BF16))[g * DH:(g + 1) * DH, :]
        psum = p[:, 0:TQ]
        for h in range(1, HG):
            psum = psum + p[:, h * TQ:(h + 1) * TQ]
        imp = sum(_dot(overlap, part) for part in _split3(psum))
        imp = jnp.where(blk * SEL_BLOCK <= t_lane, jnp.where(forced, FORCE_SCORE, imp), -1.0)
        imp_scr[g] = imp
        imps.append(imp)
        _, e = weights(s_win[g] + bias_w)
        num, den = value_sum((vw0_ref, vw1_ref)[g], w0, WL, e)
        o_rest.append(gate(g, 0) * o_cmp + gate(g, 2) * (num * (1.0 / jnp.maximum(den, 1e-30))))

    cnt_scr[...] = jnp.zeros(cnt_scr.shape, F32)
    for ib in range(NB):
        @pl.when((ib * SUBLANES * SEL_BLOCK < q0 + TQ) & (q0 + TQ > N_SELECT * SEL_BLOCK))
        def _():
            for g in G:
                cnt = [cnt_scr[g, jb * SUBLANES:(jb + 1) * SUBLANES, :] for jb in range(NB)]
                for u in range(SUBLANES):
                    r = imp_scr[g, ib * SUBLANES + u:ib * SUBLANES + u + 1, :]
                    for jb in range(NB):
                        x = imps[g][jb * SUBLANES:(jb + 1) * SUBLANES, :]
                        if jb > ib:
                            ahead = jnp.where(r >= x, 1.0, 0.0)
                        elif jb < ib:
                            ahead = jnp.where(r > x, 1.0, 0.0)
                        else:
                            ahead = jnp.where(r > x, 1.0, jnp.where((r == x) & (sub > u), 1.0, 0.0))
                        cnt[jb] = cnt[jb] + ahead
                for jb in range(NB):
                    cnt_scr[g, jb * SUBLANES:(jb + 1) * SUBLANES, :] = cnt[jb]

    groups = []
    for g in G:
        selb = jnp.where(cnt_scr[g] < float(min(N_SELECT, NSEL)), 0.0, NEG_BIG)
        if NSEL < DH:
            selb = jnp.concatenate([selb, jnp.zeros((DH - NSEL, TQ), F32)], axis=0)
        groups.append((with_half(g, per_head(selb.astype(BF16))), o_rest[g], gate(g, 1)))

    def sel_scores(g, k0, diagonal):
        s = _dot((ks0_ref, ks1_ref)[g][pl.ds(k0, TK), :], groups[g][0])
        if diagonal:
            kpos = k0 + lax.broadcasted_iota(jnp.int32, (TK, TQ), 0)
            s = s + per_head(jnp.where(kpos <= t_lane, 0.0, NEG_BIG))
        return s

    def sel_fold(g, k0, s):
        m_t, e = weights(s)
        num_t, den_t = value_sum((vs0_ref, vs1_ref)[g], k0, TK, e)
        m = m_scr[g]
        m_new = jnp.maximum(m, m_t)
        a, b = jnp.exp(m - m_new), jnp.exp(m_t - m_new)
        m_scr[g] = m_new
        den_scr[g] = a * den_scr[g] + b * den_t
        num_scr[g] = a * num_scr[g] + b * num_t

    def sel_block(tiles):
        scores = [[sel_scores(g, k0, diagonal) for g in G] for k0, diagonal in tiles]
        for (k0, _), per_group in zip(tiles, scores):
            for g, s in enumerate(per_group):
                sel_fold(g, k0, s)

    m_scr[...] = jnp.full(m_scr.shape, M_INIT, F32)
    den_scr[...] = jnp.zeros(den_scr.shape, F32)
    num_scr[...] = jnp.zeros(num_scr.shape, F32)
    n_full = q0 // TK

    def block_body(j, c):
        k0 = pl.multiple_of(j * (SEL_UNROLL * TK), SEL_UNROLL * TK)
        sel_block([(k0 + i * TK, False) for i in range(SEL_UNROLL)])
        return c

    lax.fori_loop(0, n_full // SEL_UNROLL, block_body, 0)
    k_diag = pl.multiple_of(n_full * TK, TK)

    for rem in range(SEL_UNROLL):
        @pl.when(n_full % SEL_UNROLL == rem)
        def _():
            sel_block([(pl.multiple_of(k_diag - (rem - i) * TK, TK), i == rem) for i in range(rem + 1)])

    out_rows = []
    for g, (_, o_rest, gate_sel) in enumerate(groups):
        o = o_rest + gate_sel * (num_scr[g] * (1.0 / jnp.maximum(den_scr[g], 1e-30)))
        out_rows += [o[:, h * TQ:(h + 1) * TQ] for h in range(HG)]

    o_ref[...] = _dot_tn(jnp.concatenate(out_rows, axis=0).astype(BF16), eye_ref[...]).astype(o_ref.dtype)


def _nsa(qn, sm, k_cmp, v_cmp_t, kk, vt, scl, eye, bsz, seq):
    t = qn.shape[0]
    tq = min(NSA_TQ, seq)
    nq = seq // tq
    nc = k_cmp.shape[1]
    nsel = seq // SEL_BLOCK
    assert nsel <= NSA_HEAD_DIM and nsel % SUBLANES == 0 and seq % min(NSA_TK, seq) == 0
    rows = lambda b, i: (b * nq + i, 0)
    const = lambda b, i: (0, 0)
    key_piece = lambda j: pl.BlockSpec((seq, LANES), lambda b, i: (b, j))
    val_piece = lambda j: pl.BlockSpec((LANES, seq), lambda b, i: (_VO_T // LANES + j, b))
    return pl.pallas_call(
        _nsa_kernel,
        grid=(bsz, nq),
        in_specs=[pl.BlockSpec((tq, NSA_WIDTH), rows),
                  pl.BlockSpec((tq, LANES), rows),
                  pl.BlockSpec((1, nc, NSA_KV_WIDTH), lambda b, i: (b, 0, 0)),
                  pl.BlockSpec((1, NSA_KV_WIDTH, nc), lambda b, i: (b, 0, 0)),
                  key_piece(0), key_piece(1), key_piece(2),
                  val_piece(0), val_piece(1), val_piece(2), val_piece(3),
                  pl.BlockSpec(scl.shape, const), pl.BlockSpec(eye.shape, const)],
        out_specs=pl.BlockSpec((tq, NSA_WIDTH), rows),
        out_shape=jax.ShapeDtypeStruct((t, NSA_WIDTH), BF16),
        scratch_shapes=[pltpu.VMEM((NSA_KV_GROUPS, nsel, tq), F32), pltpu.VMEM((NSA_KV_GROUPS, nsel, tq), F32),
                        pltpu.VMEM((NSA_KV_GROUPS, 1, NSA_HPG * tq), F32),
                        pltpu.VMEM((NSA_KV_GROUPS, 1, NSA_HPG * tq), F32),
                        pltpu.VMEM((NSA_KV_GROUPS, NSA_HEAD_DIM, NSA_HPG * tq), F32)],
        compiler_params=_params(("parallel", "arbitrary")),
        name="nsa",
    )(qn, sm, k_cmp, v_cmp_t, kk, kk, kk, vt, vt, vt, vt, scl, eye)


def _merge_kernel(h_ref, ada_ref, gains_ref, hm_ref, hn_ref, gt_ref, wm_ref, wn_ref, wo_ref, o_ref):
    d = h_ref.shape[1]
    _, _, gate = _mod_rows(ada_ref, 1)
    subs = [slice(r0, r0 + FFN_SUB) for r0 in range(0, h_ref.shape[0], FFN_SUB)]
    ys = [(_dot_tn(hm_ref[:, rows], wm_ref[...]),
           _dot(hn_ref[rows, :], wn_ref[...])) for rows in subs]
    zs = []
    for rows, (y_m, y_n) in zip(subs, ys):
        merged = (_sigmoid(gt_ref[rows, :d].astype(F32)) * y_m + _sigmoid(gt_ref[rows, d:].astype(F32)) * y_n)
        zs.append(_dot(merged.astype(BF16), wo_ref[...]))
    for rows, z in zip(subs, zs):
        o_ref[rows, :] = h_ref[rows, :] + gate * (_rms(z) * gains_ref[3:4, :])


def _merge(h, ada3, gains, hm, hn, gt, wm, wn, wo, seq):
    t, d = h.shape
    tm = FFN_ROWS
    per_b = seq // tm
    const = lambda i: (0, 0)
    rows = lambda i: (i, 0)
    return pl.pallas_call(
        _merge_kernel,
        grid=(t // tm,),
        in_specs=[pl.BlockSpec((tm, d), rows),
                  pl.BlockSpec((1, 9, d), lambda i: (i // per_b, 0, 0)),
                  pl.BlockSpec(gains.shape, const),
                  pl.BlockSpec((hm.shape[0], tm), lambda i: (0, i)),
                  pl.BlockSpec((tm, hn.shape[1]), rows),
                  pl.BlockSpec((tm, gt.shape[1]), rows),
                  pl.BlockSpec(wm.shape, const), pl.BlockSpec(wn.shape, const), pl.BlockSpec(wo.shape, const)],
        out_specs=pl.BlockSpec((tm, d), rows),
        out_shape=jax.ShapeDtypeStruct((t, d), F32),
        compiler_params=_params(("parallel",)),
        name="merge",
    )(h, ada3, gains, hm, hn, gt, wm, wn, wo)


def _reorder_w_in(w_in):
    m, kvw, dh = MLSTM_WIDTH, NSA_KV_WIDTH, NSA_HEAD_DIM
    o_if = 4 * m
    o_qn = o_if + _IF_COLS
    o_kc = o_qn + NSA_WIDTH
    o_vc, o_ks, o_vs, o_kw, o_vw = (o_kc + i * kvw for i in range(1, 6))
    o_gn = o_vw + kvw
    o_gm = o_gn + _GN_COLS
    zeros = lambda n: jnp.zeros((w_in.shape[0], n), w_in.dtype)
    w = jnp.concatenate([
        w_in[:, :2 * m], w_in[:, o_qn:o_kc], w_in[:, o_kc:o_vc], w_in[:, o_vc:o_ks],
        w_in[:, o_ks:o_ks + dh], zeros(dh), zeros(dh), w_in[:, o_ks + dh:o_vs], w_in[:, o_kw:o_vw],
        w_in[:, o_gm:],
        w_in[:, o_if:o_qn], w_in[:, o_gn:o_gm], zeros(_SM_W - _IF_COLS - _GN_COLS)], axis=1)
    wt = jnp.concatenate([w_in[:, 2 * m:o_if]]
                         + [piece for o in (o_vs, o_vw) for g in range(NSA_KV_GROUPS)
                            for piece in (w_in[:, o + g * dh:o + (g + 1) * dh], zeros(LANES - dh))], axis=1).T
    return w.astype(BF16), wt.astype(BF16)


def _compress_weights(cmp_pe, cmp_w1, cmp_b1, cmp_w2, cmp_b2):
    G, dh, hid = NSA_KV_GROUPS, NSA_HEAD_DIM, CMP_HIDDEN
    eye = jnp.eye(G, dtype=F32)
    w1 = cmp_w1.astype(BF16).reshape(2, CMP_BLOCK, 1, dh, hid)
    zero = jnp.zeros_like(w1)
    w1 = jnp.concatenate([jnp.concatenate([w1 if g == h else zero for h in range(G)], axis=-1)
                          for g in range(G)], axis=2)
    w1 = w1.reshape(2, CMP_BLOCK * G * dh, G * hid)
    half = CMP_STRIDE * G * dh
    wa, wb = w1[:, :half], w1[:, half:]
    pe = jnp.broadcast_to(cmp_pe[:, :, None, :], (2, CMP_BLOCK, G, dh)).reshape(2, 1, CMP_BLOCK * G * dh)
    pe = jnp.broadcast_to(pe, (2, 8, CMP_BLOCK * G * dh)).astype(BF16)
    pea, peb = pe[:, :, :half], pe[:, :, half:]
    b1 = jnp.tile(cmp_b1, (1, G)).reshape(2, 1, G * hid)
    w2 = jnp.einsum('kjd,gh->kgjhd', cmp_w2, eye).reshape(2, G * hid, G * dh).astype(BF16)
    b2 = jnp.tile(cmp_b2, (1, G))
    return wa, wb, pea, peb, b1, w2[0], b2[0].reshape(1, G * dh), w2[1].T, b2[1].reshape(G * dh, 1)


def kernel(x, c, w_ada, b_ada, norm_gains, ffn_wg, ffn_wu, ffn_wd, w_in, mlstm_conv_w, mlstm_conv_b,
           mlstm_gate_b, mlstm_head_gain, cmp_pe, cmp_w1, cmp_b1, cmp_w2, cmp_b2, w_up_mlstm, w_up_nsa, w_out):
    bsz, seq, d = x.shape
    t = bsz * seq
    h = x.reshape(t, d)
    eye = jnp.eye(NSA_WIDTH, dtype=BF16)
    scl = (jnp.eye(NSA_WIDTH, dtype=F32) * NSA_HEAD_DIM ** -0.5).astype(BF16)
    for l in range(w_ada.shape[0]):
        gains = norm_gains[l]
        ada3 = _ada(c, w_ada[l], b_ada[l]).reshape(bsz, 9, d)
        wg, wu, wd = ffn_wg[l].astype(BF16), ffn_wu[l].astype(BF16), ffn_wd[l].astype(BF16)

        h = _ffn(h, ada3, gains, wg[0], wu[0], wd[0], 0, 0.5, seq)

        qk, qn, kc, vc, kk, gt, sm, vt = _inproj(h, ada3, gains, *_reorder_w_in(w_in[l]), seq)
        gate_b_row = jnp.concatenate([mlstm_gate_b[l], jnp.zeros((LANES - _IF_COLS,), F32)]).reshape(1, LANES)
        hm = _mlstm(qk, vt, sm, _shift_matrices(min(MLSTM_L, seq), 16),
                    mlstm_conv_w[l].reshape(CONV_WIDTH, 2 * MLSTM_WIDTH), mlstm_conv_b[l].reshape(1, -1),
                    gate_b_row, mlstm_head_gain[l].reshape(-1, 1), bsz, seq)

        chunked = (bsz, seq // CMP_STRIDE, CMP_STRIDE * NSA_KV_WIDTH)
        k_cmp, v_cmp_t = _compress(kc.reshape(chunked), vc.reshape(chunked),
                                   *_compress_weights(cmp_pe[l], cmp_w1[l], cmp_b1[l], cmp_w2[l], cmp_b2[l]))
        hn = _nsa(qn, sm, k_cmp, v_cmp_t, kk, vt, scl, eye, bsz, seq)

        h = _merge(h, ada3, gains, hm, hn, gt, w_up_mlstm[l].astype(BF16), w_up_nsa[l].astype(BF16),
                   w_out[l].astype(BF16), seq)
        h = _ffn(h, ada3, gains, wg[1], wu[1], wd[1], 2, 0.5, seq)
    return h.reshape(bsz, seq, d)
```

```python
import functools

import jax
import jax.numpy as jnp
from jax import lax
from jax.experimental import pallas as pl
from jax.experimental.pallas import tpu as pltpu

D_MODEL = 1024
D_FF = 2816
NORM_EPS = 1e-6
CONV_WIDTH = 4
MLSTM_HEADS = 4
MLSTM_HEAD_DIM = 128
MLSTM_WIDTH = MLSTM_HEADS * MLSTM_HEAD_DIM
NSA_HEADS = 8
NSA_KV_GROUPS = 2
NSA_HPG = NSA_HEADS // NSA_KV_GROUPS
NSA_HEAD_DIM = 64
NSA_WIDTH = NSA_HEADS * NSA_HEAD_DIM
NSA_KV_WIDTH = NSA_KV_GROUPS * NSA_HEAD_DIM
CMP_BLOCK = 32
CMP_STRIDE = 16
CMP_HIDDEN = 256
SEL_BLOCK = 64
N_SELECT = 16
WINDOW = 256
FORCE_SCORE = 1e4
NEG_BIG = -1e30
M_INIT = -1e29

LANES = 128
SUBLANES = 8
VMEM_LIMIT = 56 * 1024 * 1024

FFN_SUB = 512
FFN_ROWS = 1024
FFN_FC = 256
MLSTM_L = 256
NSA_TQ = 256
SEL_UNROLL = 4

BF16 = jnp.bfloat16
F32 = jnp.float32


def _params(sem):
    return pltpu.CompilerParams(dimension_semantics=sem, vmem_limit_bytes=VMEM_LIMIT)


def _rms(y):
    return y * lax.rsqrt(jnp.mean(y * y, axis=-1, keepdims=True) + NORM_EPS)


def _sigmoid(x):
    return 0.5 * jnp.tanh(0.5 * x) + 0.5


def _dot(a, b):
    return jnp.dot(a, b, preferred_element_type=F32)


def _dot_nt(a, b, precision=None):
    return lax.dot_general(a, b, (((1,), (1,)), ((), ())), preferred_element_type=F32, precision=precision)


def _dot_tn(a, b):
    return lax.dot_general(a, b, (((0,), (0,)), ((), ())), preferred_element_type=F32)


def _ada_kernel(c_ref, w_ref, b_ref, o_ref):
    c = c_ref[...]
    sc = c * _sigmoid(c)
    o_ref[...] = jnp.dot(sc, w_ref[...], preferred_element_type=F32,
                         precision=lax.Precision.HIGHEST) + b_ref[...]


def _ada(c, w, b):
    bsz, d = c.shape
    n = w.shape[1]
    tn = 1152
    return pl.pallas_call(
        _ada_kernel,
        grid=(n // tn,),
        in_specs=[pl.BlockSpec((bsz, d), lambda j: (0, 0)),
                  pl.BlockSpec((d, tn), lambda j: (0, j)),
                  pl.BlockSpec((1, tn), lambda j: (0, j))],
        out_specs=pl.BlockSpec((bsz, tn), lambda j: (0, j)),
        out_shape=jax.ShapeDtypeStruct((bsz, n), F32),
        compiler_params=_params(("arbitrary",)),
        name="ada",
    )(c, w, b.reshape(1, n))


def _mod_rows(ada_ref, s):
    return (ada_ref[0, 3 * s:3 * s + 1, :], ada_ref[0, 3 * s + 1:3 * s + 2, :],
            ada_ref[0, 3 * s + 2:3 * s + 3, :])


def _ffn_kernel(h_ref, ada_ref, gains_ref, wg_ref, wu_ref, wd_ref, o_ref, *, s, resid):
    shift, scale, gate = _mod_rows(ada_ref, s)
    subs = [slice(r0, r0 + FFN_SUB) for r0 in range(0, h_ref.shape[0], FFN_SUB)]
    us = [(_rms(h_ref[rows, :]) * gains_ref[2 * s:2 * s + 1, :] * (1.0 + scale) + shift).astype(BF16)
          for rows in subs]
    accs = [jnp.zeros((FFN_SUB, h_ref.shape[1]), F32) for _ in subs]
    for c0 in range(0, D_FF, FFN_FC):
        for i, u in enumerate(us):
            g = _dot(u, wg_ref[:, c0:c0 + FFN_FC])
            up = _dot(u, wu_ref[:, c0:c0 + FFN_FC])
            hid = (g * _sigmoid(g) * up).astype(BF16)
            accs[i] = accs[i] + _dot(hid, wd_ref[c0:c0 + FFN_FC, :])
    for rows, acc in zip(subs, accs):
        post = _rms(acc) * gains_ref[2 * s + 1:2 * s + 2, :]
        o_ref[rows, :] = h_ref[rows, :] + resid * gate * post


def _ffn(h, ada3, gains, wg, wu, wd, s, resid, seq):
    t, d = h.shape
    tm = FFN_ROWS
    per_b = seq // tm
    const = lambda i: (0, 0)
    return pl.pallas_call(
        functools.partial(_ffn_kernel, s=s, resid=resid),
        grid=(t // tm,),
        in_specs=[pl.BlockSpec((tm, d), lambda i: (i, 0)),
                  pl.BlockSpec((1, 9, d), lambda i: (i // per_b, 0, 0)),
                  pl.BlockSpec(gains.shape, const),
                  pl.BlockSpec(wg.shape, const, pipeline_mode=pl.Buffered(1)),
                  pl.BlockSpec(wu.shape, const, pipeline_mode=pl.Buffered(1)),
                  pl.BlockSpec(wd.shape, const, pipeline_mode=pl.Buffered(1))],
        out_specs=pl.BlockSpec((tm, d), lambda i: (i, 0)),
        out_shape=jax.ShapeDtypeStruct((t, d), F32),
        compiler_params=_params(("parallel",)),
        name=f"ffn{s}",
    )(h, ada3, gains, wg, wu, wd)


_QK_W = 2 * MLSTM_WIDTH
_KC_W = NSA_KV_WIDTH
_VC_W = NSA_KV_WIDTH
_KK_W = 3 * LANES
_GT_W = 2 * D_MODEL
_SM_W = LANES
_VO_T = 2 * MLSTM_WIDTH
_VN_T = 4 * LANES
_QN_T = NSA_WIDTH
_VT_W = _VO_T + _VN_T + _QN_T
_IF_COLS = 2 * MLSTM_HEADS
_GN_COLS = 3 * NSA_HEADS
_IN_WIDTHS = (_QK_W, _KC_W, _VC_W, _KK_W, _GT_W, _SM_W)
_IN_DTYPES = (BF16, BF16, BF16, BF16, BF16, F32)


def _inproj_kernel(h_ref, ada_ref, gains_ref, w_ref, wt_ref, qk_ref, kc_ref, vc_ref, kk_ref,
                   gt_ref, sm_ref, vt_ref, *, per_b):
    tm = h_ref.shape[0]
    shift, scale, _ = _mod_rows(ada_ref, 1)
    subs = [slice(r0, r0 + FFN_SUB) for r0 in range(0, tm, FFN_SUB)]
    us = [(_rms(h_ref[rows, :]) * gains_ref[2:3, :] * (1.0 + scale) + shift).astype(BF16) for rows in subs]
    step = 512
    off = 0
    for ref in (qk_ref, kc_ref, vc_ref, kk_ref, gt_ref, sm_ref):
        width = ref.shape[1]
        for c0 in range(0, width, step):
            c1 = min(c0 + step, width)
            for rows, u in zip(subs, us):
                res = _dot(u, w_ref[:, off + c0:off + c1])
                if ref is kk_ref and c0 == 0:
                    pos = ((pl.program_id(0) % per_b) * tm + rows.start
                           + lax.broadcasted_iota(jnp.int32, (FFN_SUB, c1 - c0), 0))
                    lane = lax.broadcasted_iota(jnp.int32, (FFN_SUB, c1 - c0), 1)
                    hot = (lane == NSA_HEAD_DIM + pos // SEL_BLOCK) | (lane == LANES + pos // SEL_BLOCK)
                    res = res + jnp.where(hot, 1.0, 0.0)
                ref[rows, c0:c1] = res.astype(ref.dtype)
        off += width
    ones_row = lax.broadcasted_iota(jnp.int32, (_VN_T, FFN_SUB), 0) % LANES == NSA_HEAD_DIM
    for rows, u in zip(subs, us):
        vt_ref[0:_VO_T, rows] = _dot_nt(wt_ref[0:_VO_T, :], u).astype(vt_ref.dtype)
        vt_ref[_VO_T:_VO_T + _VN_T, rows] = (_dot_nt(wt_ref[_VO_T:_VO_T + _VN_T, :], u)
                                             + jnp.where(ones_row, 1.0, 0.0)).astype(vt_ref.dtype)
        vt_ref[_VO_T + _VN_T:, rows] = (_dot_nt(wt_ref[_VO_T + _VN_T:, :], u)
                                        * NSA_HEAD_DIM ** -0.5).astype(vt_ref.dtype)


def _inproj(h, ada3, gains, w, wt, seq):
    t, d = h.shape
    tm = FFN_ROWS
    per_b = seq // tm
    const = lambda i: (0, 0)
    return pl.pallas_call(
        functools.partial(_inproj_kernel, per_b=per_b),
        grid=(t // tm,),
        in_specs=[pl.BlockSpec((tm, d), lambda i: (i, 0)),
                  pl.BlockSpec((1, 9, d), lambda i: (i // per_b, 0, 0)),
                  pl.BlockSpec(gains.shape, const),
                  pl.BlockSpec(w.shape, const, pipeline_mode=pl.Buffered(1)),
                  pl.BlockSpec(wt.shape, const, pipeline_mode=pl.Buffered(1))],
        out_specs=[pl.BlockSpec((tm, wd), lambda i: (i, 0)) for wd in _IN_WIDTHS]
        + [pl.BlockSpec((_VT_W, tm), lambda i: (0, i))],
        out_shape=[jax.ShapeDtypeStruct((t, wd), dt) for wd, dt in zip(_IN_WIDTHS, _IN_DTYPES)]
        + [jax.ShapeDtypeStruct((_VT_W, t), BF16)],
        compiler_params=_params(("parallel",)),
        name="inproj",
    )(h, ada3, gains, w, wt)


_MX_ROWS = MLSTM_HEAD_DIM + 16


def _split3(x):
    hi = x.astype(BF16)
    r1 = x - hi.astype(F32)
    mid = r1.astype(BF16)
    return hi, mid, (r1 - mid.astype(F32)).astype(BF16)


def _mlstm_kernel(qk_ref, vot_ref, sm_ref, shift_ref, cw_ref, cb_ref, gb_ref, hg_ref, o_ref, tail, cext, mscr):
    L = qk_ref.shape[0]
    W = MLSTM_WIDTH
    dh = MLSTM_HEAD_DIM
    H = MLSTM_HEADS
    TAIL = tail.shape[0]

    @pl.when(pl.program_id(1) == 0)
    def _():
        tail[...] = jnp.zeros(tail.shape, tail.dtype)
        cext[...] = jnp.zeros(cext.shape, F32)
        mscr[...] = jnp.zeros(mscr.shape, F32)

    x = qk_ref[...]
    prev = tail[...]
    conv = cb_ref[...] + cw_ref[CONV_WIDTH - 1:CONV_WIDTH, :] * x.astype(F32)
    head = jnp.zeros((TAIL, 2 * W), F32)
    for j in range(CONV_WIDTH - 1):
        conv = conv + cw_ref[j:j + 1, :] * _dot(shift_ref[j], x)
        head = head + cw_ref[j:j + 1, :] * _dot(shift_ref[CONV_WIDTH - 1 + j, 0:TAIL, 0:TAIL], prev)
    conv = jnp.concatenate([conv[0:TAIL, :] + head, conv[TAIL:, :]], axis=0)
    tail[...] = x[L - TAIL:L, :]
    qk = conv * _sigmoid(conv)

    gts = sm_ref[...] + gb_ref[...]
    logf = jnp.minimum(gts, 0.0) - jnp.log1p(jnp.exp(-jnp.abs(gts)))
    s_idx = lax.broadcasted_iota(jnp.int32, (L, L), 0)
    t_idx = lax.broadcasted_iota(jnp.int32, (L, L), 1)
    causal = s_idx <= t_idx
    tri = jnp.where(causal, 1.0, 0.0).astype(BF16)
    logf_t = logf.T
    b_cols = sum(_dot_tn(tri, part) for part in _split3(logf))
    b_rows = sum(_dot(part, tri) for part in _split3(logf_t))
    g_rows = gts.T
    ones_tile = (lax.broadcasted_iota(jnp.int32, (_MX_ROWS - dh, L), 0) == 0).astype(BF16)

    for h in range(H):
        q = qk[:, h * dh:(h + 1) * dh].astype(BF16)
        k = (qk[:, W + h * dh:W + (h + 1) * dh] * (dh ** -0.5)).astype(BF16)
        vext = jnp.concatenate([vot_ref[h * dh:(h + 1) * dh, :], ones_tile], axis=0)
        b_row = b_rows[H + h:H + h + 1, :]
        i_row = g_rows[h:h + 1, :]
        r_col = gts[:, h:h + 1] - b_cols[:, H + h:H + h + 1]
        m_prev = mscr[h:h + 1, 0:1]
        c_old = cext[h]

        ld = jnp.where(causal, r_col + b_row, NEG_BIG)
        a = b_row + m_prev
        m_t = jnp.maximum(a, jnp.max(ld, axis=0, keepdims=True))
        p = (_dot_nt(k, q) * jnp.exp(ld - m_t)).astype(BF16)
        w_inter = jnp.exp(a - m_t)
        nd = _dot(vext, p) + w_inter * _dot_nt(c_old.astype(BF16), q)
        hh = nd[0:dh, :] * (1.0 / jnp.maximum(jnp.abs(nd[dh:dh + 1, :]), jnp.exp(-m_t)))

        b_last = b_row[:, L - 1:L]
        g_row = b_last - b_row + i_row
        m_new = jnp.maximum(b_last + m_prev, jnp.max(g_row, axis=1, keepdims=True))
        vw = (vext.astype(F32) * jnp.exp(g_row - m_new)).astype(BF16)
        cext[h] = jnp.exp(b_last + m_prev - m_new) * c_old + _dot(vw, k)
        mscr[h:h + 1, :] = jnp.broadcast_to(m_new, (1, LANES))

        hn = hh * lax.rsqrt(jnp.mean(hh * hh, axis=0, keepdims=True) + NORM_EPS) * hg_ref[h * dh:(h + 1) * dh, :]
        o_pre = vot_ref[W + h * dh:W + (h + 1) * dh, :].astype(F32)
        o_ref[h * dh:(h + 1) * dh, :] = (_sigmoid(o_pre) * hn).astype(o_ref.dtype)


def _mlstm(qk, vt, sm, shifts, conv_w, conv_b, gate_b_row, head_gain_col, bsz, seq):
    t = qk.shape[0]
    L = min(MLSTM_L, seq)
    nc = seq // L
    W = MLSTM_WIDTH
    const = lambda b, c: (0, 0)
    rows = lambda b, c: (b * nc + c, 0)
    cols = lambda b, c: (0, b * nc + c)
    return pl.pallas_call(
        _mlstm_kernel,
        grid=(bsz, nc),
        in_specs=[pl.BlockSpec((L, 2 * W), rows),
                  pl.BlockSpec((_VO_T, L), cols),
                  pl.BlockSpec((L, LANES), rows),
                  pl.BlockSpec(shifts.shape, lambda b, c: (0, 0, 0)),
                  pl.BlockSpec(conv_w.shape, const),
                  pl.BlockSpec(conv_b.shape, const),
                  pl.BlockSpec(gate_b_row.shape, const),
                  pl.BlockSpec(head_gain_col.shape, const)],
        out_specs=pl.BlockSpec((W, L), cols),
        out_shape=jax.ShapeDtypeStruct((W, t), BF16),
        scratch_shapes=[pltpu.VMEM((16, 2 * W), BF16),
                        pltpu.VMEM((MLSTM_HEADS, _MX_ROWS, MLSTM_HEAD_DIM), F32),
                        pltpu.VMEM((8, LANES), F32)],
        compiler_params=_params(("parallel", "arbitrary")),
        name="mlstm",
    )(qk, vt, sm, shifts, conv_w, conv_b, gate_b_row, head_gain_col)


def _shift_matrices(L, tail):
    t_idx = jnp.arange(L)[:, None]
    s_idx = jnp.arange(L)[None, :]
    mats = [(s_idx == t_idx - (CONV_WIDTH - 1 - j)) for j in range(CONV_WIDTH - 1)]
    mats += [(s_idx - tail == t_idx - (CONV_WIDTH - 1 - j)) & (s_idx < tail) & (t_idx < tail)
             for j in range(CONV_WIDTH - 1)]
    return jnp.stack(mats).astype(BF16)


def _compress_kernel(kc_ref, vc_ref, wa_ref, wb_ref, pea_ref, peb_ref, b1_ref, w2k_ref, b2k_ref, w2vt_ref, b2vt_ref,
                     ko_ref, vto_ref):
    def hidden(i, x):
        wa, wb = wa_ref[i], wb_ref[i]
        bias = _dot(pea_ref[i], wa) + _dot(peb_ref[i], wb) + b1_ref[i]
        pre = _dot(x, wa) + pltpu.roll(_dot(x, wb), x.shape[0] - 1, 0) + bias[0:1, :]
        return (pre * _sigmoid(pre)).astype(BF16)

    ko_ref[0] = (_dot(hidden(0, kc_ref[0]), w2k_ref[...]) + b2k_ref[...]).astype(ko_ref.dtype)
    vto_ref[0] = (_dot_nt(w2vt_ref[...], hidden(1, vc_ref[0])) + b2vt_ref[...]).astype(vto_ref.dtype)


def _compress(kc, vc, wa, wb, pea, peb, b1, w2k, b2k, w2vt, b2vt):
    bsz, n, width = kc.shape
    full = lambda a: pl.BlockSpec(a.shape, lambda b: (0,) * a.ndim)
    per_b = pl.BlockSpec((1, n, width), lambda b: (b, 0, 0))
    return pl.pallas_call(
        _compress_kernel,
        grid=(bsz,),
        in_specs=[per_b, per_b] + [full(a) for a in (wa, wb, pea, peb, b1, w2k, b2k, w2vt, b2vt)],
        out_specs=[pl.BlockSpec((1, n, NSA_KV_WIDTH), lambda b: (b, 0, 0)),
                   pl.BlockSpec((1, NSA_KV_WIDTH, n), lambda b: (b, 0, 0))],
        out_shape=[jax.ShapeDtypeStruct((bsz, n, NSA_KV_WIDTH), BF16),
                   jax.ShapeDtypeStruct((bsz, NSA_KV_WIDTH, n), BF16)],
        compiler_params=_params(("parallel",)),
        name="compress",
    )(kc, vc, wa, wb, pea, peb, b1, w2k, b2k, w2vt, b2vt)


def _softmax_cols(s):
    m = jnp.maximum(jnp.max(s, axis=0, keepdims=True), M_INIT)
    e = jnp.exp(s - m)
    return e * (1.0 / jnp.maximum(jnp.sum(e, axis=0, keepdims=True), 1e-30))


def _nsa_kernel(qt_ref, sm_ref, kc_ref, vct_ref, ks0_ref, ks1_ref, kw_ref, vs0_ref, vs1_ref, vw0_ref, vw1_ref,
                eye_ref, o_ref, imp_scr, cnt_scr, m_scr, den_scr, num_scr):
    TQ = qt_ref.shape[1]
    S = kw_ref.shape[0]
    NC = kc_ref.shape[1]
    TK = TQ
    NSEL = S // SEL_BLOCK
    NB = NSEL // SUBLANES
    HG = NSA_HPG
    DH = NSA_HEAD_DIM
    R = HG * TQ
    HQ = TQ // 2
    HR = HG * HQ
    WL = min(WINDOW + HQ, S)
    q0 = pl.program_id(1) * TQ

    qt_all = qt_ref[...]
    gates_t = _sigmoid(sm_ref[...]).T
    t_lane = q0 + lax.broadcasted_iota(jnp.int32, (1, TQ), 1)
    zeros_half = jnp.zeros((DH, R), BF16)
    sub = lax.broadcasted_iota(jnp.int32, (SUBLANES, TQ), 0)

    def lanes(f):
        xs = [f(h) for h in range(HG)]
        return jnp.concatenate([x[:, c * HQ:(c + 1) * HQ] for c in range(2) for x in xs], axis=1)

    def per_head(x):
        return lanes(lambda h: x)

    def half_heads(x):
        return jnp.concatenate([x] * HG, axis=1)

    def head_cols(x, h):
        return jnp.concatenate([x[:, (c * HG + h) * HQ:(c * HG + h + 1) * HQ] for c in range(2)], axis=1)

    def weights(s):
        m = jnp.maximum(jnp.max(s, axis=0, keepdims=True), M_INIT)
        return m, jnp.exp((s - m).astype(BF16))

    def value_sum(v_ref, k0, n, e):
        nd = _dot(v_ref[0:DH + 16, pl.ds(k0, n)], e.astype(BF16))
        return nd[0:DH, :], nd[DH:DH + 1, :]

    G = range(NSA_KV_GROUPS)
    qts = [lanes(lambda h: qt_all[(g * HG + h) * DH:(g * HG + h + 1) * DH, :]) for g in G]

    def with_half(g, other):
        return jnp.concatenate([qts[g], other] if g == 0 else [other, qts[g]], axis=0)

    def gate(g, br):
        return lanes(lambda h: gates_t[_IF_COLS + (g * HG + h) * 3 + br:
                                       _IF_COLS + (g * HG + h) * 3 + br + 1, :])

    qt_plain = [with_half(g, zeros_half) for g in G]
    halves = [(c, slice(c * HR, (c + 1) * HR)) for c in range(2)]
    w0 = [pl.multiple_of(jnp.maximum(q0 + (c + 1) * HQ - WL, 0), LANES) for c in range(2)]
    s_cmp = [_dot(kc_ref[0], qt_plain[g]) for g in G]
    s_win = [[_dot(kw_ref[pl.ds(w0[c], WL), :], qt_plain[g][:, cols]) for c, cols in halves] for g in G]
    cmp_end = lax.broadcasted_iota(jnp.int32, (NC, TQ), 0) * CMP_STRIDE + (CMP_BLOCK - 1)
    bias_c = per_head(jnp.where(cmp_end <= t_lane, 0.0, NEG_BIG))
    bias_w = []
    for c in range(2):
        wpos = w0[c] + lax.broadcasted_iota(jnp.int32, (WL, HQ), 0)
        t_half = t_lane[:, c * HQ:(c + 1) * HQ]
        bias_w.append(half_heads(jnp.where((wpos <= t_half) & (wpos > t_half - WINDOW), 0.0, NEG_BIG)))
    jj = lax.broadcasted_iota(jnp.int32, (NSEL, NC), 0) * SEL_BLOCK
    nn = lax.broadcasted_iota(jnp.int32, (NSEL, NC), 1) * CMP_STRIDE
    overlap = jnp.where((nn < jj + SEL_BLOCK) & (nn + CMP_BLOCK > jj), 1.0, 0.0).astype(BF16)
    blk = lax.broadcasted_iota(jnp.int32, (NSEL, TQ), 0)
    cur = t_lane // SEL_BLOCK
    forced = (blk == 0) | (blk == cur) | (blk == cur - 1)

    o_rest, imps = [], []
    for g in G:
        p = _softmax_cols(s_cmp[g] + bias_c)
        o_cmp = _dot(vct_ref[0], p.astype(BF16))[g * DH:(g + 1) * DH, :]
        psum = head_cols(p, 0)
        for h in range(1, HG):
            psum = psum + head_cols(p, h)
        imp = sum(_dot(overlap, part) for part in _split3(psum))
        imp = jnp.where(blk * SEL_BLOCK <= t_lane, jnp.where(forced, FORCE_SCORE, imp), -1.0)
        imp_scr[g] = imp
        imps.append(imp)
        o_win = []
        for c, _ in halves:
            _, e = weights(s_win[g][c] + bias_w[c])
            num, den = value_sum((vw0_ref, vw1_ref)[g], w0[c], WL, e)
            o_win.append(num * (1.0 / jnp.maximum(den, 1e-30)))
        o_rest.append(gate(g, 0) * o_cmp + gate(g, 2) * jnp.concatenate(o_win, axis=1))

    cnt_scr[...] = jnp.zeros(cnt_scr.shape, F32)
    for ib in range(NB):
        @pl.when((ib * SUBLANES * SEL_BLOCK < q0 + TQ) & (q0 + TQ > N_SELECT * SEL_BLOCK))
        def _():
            for g in G:
                cnt = [cnt_scr[g, jb * SUBLANES:(jb + 1) * SUBLANES, :] for jb in range(NB)]
                for u in range(SUBLANES):
                    r = imp_scr[g, ib * SUBLANES + u:ib * SUBLANES + u + 1, :]
                    for jb in range(NB):
                        x = imps[g][jb * SUBLANES:(jb + 1) * SUBLANES, :]
                        if jb > ib:
                            ahead = jnp.where(r >= x, 1.0, 0.0)
                        elif jb < ib:
                            ahead = jnp.where(r > x, 1.0, 0.0)
                        else:
                            ahead = jnp.where(r > x, 1.0, jnp.where((r == x) & (sub > u), 1.0, 0.0))
                        cnt[jb] = cnt[jb] + ahead
                for jb in range(NB):
                    cnt_scr[g, jb * SUBLANES:(jb + 1) * SUBLANES, :] = cnt[jb]

    groups = []
    for g in G:
        selb = jnp.where(cnt_scr[g] < float(min(N_SELECT, NSEL)), 0.0, NEG_BIG)
        if NSEL < DH:
            selb = jnp.concatenate([selb, jnp.zeros((DH - NSEL, TQ), F32)], axis=0)
        groups.append((with_half(g, per_head(selb.astype(BF16))), o_rest[g], gate(g, 1)))

    def sel_scores(g, k0, diagonal):
        if not diagonal:
            return [(slice(0, R), TK, _dot((ks0_ref, ks1_ref)[g][pl.ds(k0, TK), :], groups[g][0]))]
        pieces = []
        for c, cols in halves:
            n = (c + 1) * HQ
            kpos = k0 + lax.broadcasted_iota(jnp.int32, (n, HQ), 0)
            bias = half_heads(jnp.where(kpos <= t_lane[:, c * HQ:(c + 1) * HQ], 0.0, NEG_BIG))
            pieces.append((cols, n, _dot((ks0_ref, ks1_ref)[g][pl.ds(k0, n), :], groups[g][0][:, cols]) + bias))
        return pieces

    def sel_fold(g, k0, cols, n, s):
        m_t, e = weights(s)
        num_t, den_t = value_sum((vs0_ref, vs1_ref)[g], k0, n, e)
        m = m_scr[g, :, cols]
        m_new = jnp.maximum(m, m_t)
        a, b = jnp.exp(m - m_new), jnp.exp(m_t - m_new)
        m_scr[g, :, cols] = m_new
        den_scr[g, :, cols] = a * den_scr[g, :, cols] + b * den_t
        num_scr[g, :, cols] = a * num_scr[g, :, cols] + b * num_t

    def sel_block(tiles):
        scores = [[sel_scores(g, k0, diagonal) for g in G] for k0, diagonal in tiles]
        for (k0, _), per_group in zip(tiles, scores):
            for g, pieces in enumerate(per_group):
                for cols, n, s in pieces:
                    sel_fold(g, k0, cols, n, s)

    m_scr[...] = jnp.full(m_scr.shape, M_INIT, F32)
    den_scr[...] = jnp.zeros(den_scr.shape, F32)
    num_scr[...] = jnp.zeros(num_scr.shape, F32)
    n_full = q0 // TK

    def block_body(j, c):
        k0 = pl.multiple_of(j * (SEL_UNROLL * TK), SEL_UNROLL * TK)
        sel_block([(k0 + i * TK, False) for i in range(SEL_UNROLL)])
        return c

    lax.fori_loop(0, n_full // SEL_UNROLL, block_body, 0)
    k_diag = pl.multiple_of(n_full * TK, TK)

    for rem in range(SEL_UNROLL):
        @pl.when(n_full % SEL_UNROLL == rem)
        def _():
            sel_block([(pl.multiple_of(k_diag - (rem - i) * TK, TK), i == rem) for i in range(rem + 1)])

    out_rows = []
    for g, (_, o_rest, gate_sel) in enumerate(groups):
        o = o_rest + gate_sel * (num_scr[g] * (1.0 / jnp.maximum(den_scr[g], 1e-30)))
        out_rows += [head_cols(o, h) for h in range(HG)]

    o_ref[...] = _dot_tn(jnp.concatenate(out_rows, axis=0).astype(BF16), eye_ref[...]).astype(o_ref.dtype)


def _nsa(sm, k_cmp, v_cmp_t, kk, vt, eye, bsz, seq):
    t = sm.shape[0]
    tq = min(NSA_TQ, seq)
    nq = seq // tq
    nc = k_cmp.shape[1]
    nsel = seq // SEL_BLOCK
    assert nsel <= NSA_HEAD_DIM and nsel % SUBLANES == 0 and tq % (2 * LANES) == 0
    rows = lambda b, i: (b * nq + i, 0)
    const = lambda b, i: (0, 0)
    key_piece = lambda j: pl.BlockSpec((seq, LANES), lambda b, i: (b, j))
    val_piece = lambda j: pl.BlockSpec((LANES, seq), lambda b, i: (_VO_T // LANES + j, b))
    return pl.pallas_call(
        _nsa_kernel,
        grid=(bsz, nq),
        in_specs=[pl.BlockSpec((_QN_T, tq), lambda b, i: ((_VO_T + _VN_T) // _QN_T, b * nq + i)),
                  pl.BlockSpec((tq, LANES), rows),
                  pl.BlockSpec((1, nc, NSA_KV_WIDTH), lambda b, i: (b, 0, 0)),
                  pl.BlockSpec((1, NSA_KV_WIDTH, nc), lambda b, i: (b, 0, 0)),
                  key_piece(0), key_piece(1), key_piece(2),
                  val_piece(0), val_piece(1), val_piece(2), val_piece(3),
                  pl.BlockSpec(eye.shape, const)],
        out_specs=pl.BlockSpec((tq, NSA_WIDTH), rows),
        out_shape=jax.ShapeDtypeStruct((t, NSA_WIDTH), BF16),
        scratch_shapes=[pltpu.VMEM((NSA_KV_GROUPS, nsel, tq), F32), pltpu.VMEM((NSA_KV_GROUPS, nsel, tq), F32),
                        pltpu.VMEM((NSA_KV_GROUPS, 1, NSA_HPG * tq), F32),
                        pltpu.VMEM((NSA_KV_GROUPS, 1, NSA_HPG * tq), F32),
                        pltpu.VMEM((NSA_KV_GROUPS, NSA_HEAD_DIM, NSA_HPG * tq), F32)],
        compiler_params=_params(("parallel", "arbitrary")),
        name="nsa",
    )(vt, sm, k_cmp, v_cmp_t, kk, kk, kk, vt, vt, vt, vt, eye)


def _merge_kernel(h_ref, ada_ref, gains_ref, hm_ref, hn_ref, gt_ref, wm_ref, wn_ref, wo_ref, o_ref):
    d = h_ref.shape[1]
    _, _, gate = _mod_rows(ada_ref, 1)
    subs = [slice(r0, r0 + FFN_SUB) for r0 in range(0, h_ref.shape[0], FFN_SUB)]
    ys = [(_dot_tn(hm_ref[:, rows], wm_ref[...]),
           _dot(hn_ref[rows, :], wn_ref[...])) for rows in subs]
    zs = []
    for rows, (y_m, y_n) in zip(subs, ys):
        merged = (_sigmoid(gt_ref[rows, :d].astype(F32)) * y_m + _sigmoid(gt_ref[rows, d:].astype(F32)) * y_n)
        zs.append(_dot(merged.astype(BF16), wo_ref[...]))
    for rows, z in zip(subs, zs):
        o_ref[rows, :] = h_ref[rows, :] + gate * (_rms(z) * gains_ref[3:4, :])


def _merge(h, ada3, gains, hm, hn, gt, wm, wn, wo, seq):
    t, d = h.shape
    tm = FFN_ROWS
    per_b = seq // tm
    const = lambda i: (0, 0)
    rows = lambda i: (i, 0)
    return pl.pallas_call(
        _merge_kernel,
        grid=(t // tm,),
        in_specs=[pl.BlockSpec((tm, d), rows),
                  pl.BlockSpec((1, 9, d), lambda i: (i // per_b, 0, 0)),
                  pl.BlockSpec(gains.shape, const),
                  pl.BlockSpec((hm.shape[0], tm), lambda i: (0, i)),
                  pl.BlockSpec((tm, hn.shape[1]), rows),
                  pl.BlockSpec((tm, gt.shape[1]), rows),
                  pl.BlockSpec(wm.shape, const), pl.BlockSpec(wn.shape, const), pl.BlockSpec(wo.shape, const)],
        out_specs=pl.BlockSpec((tm, d), rows),
        out_shape=jax.ShapeDtypeStruct((t, d), F32),
        compiler_params=_params(("parallel",)),
        name="merge",
    )(h, ada3, gains, hm, hn, gt, wm, wn, wo)


def _reorder_w_in(w_in):
    m, kvw, dh = MLSTM_WIDTH, NSA_KV_WIDTH, NSA_HEAD_DIM
    o_if = 4 * m
    o_qn = o_if + _IF_COLS
    o_kc = o_qn + NSA_WIDTH
    o_vc, o_ks, o_vs, o_kw, o_vw = (o_kc + i * kvw for i in range(1, 6))
    o_gn = o_vw + kvw
    o_gm = o_gn + _GN_COLS
    zeros = lambda n: jnp.zeros((w_in.shape[0], n), w_in.dtype)
    w = jnp.concatenate([
        w_in[:, :2 * m], w_in[:, o_kc:o_vc], w_in[:, o_vc:o_ks],
        w_in[:, o_ks:o_ks + dh], zeros(dh), zeros(dh), w_in[:, o_ks + dh:o_vs], w_in[:, o_kw:o_vw],
        w_in[:, o_gm:],
        w_in[:, o_if:o_qn], w_in[:, o_gn:o_gm], zeros(_SM_W - _IF_COLS - _GN_COLS)], axis=1)
    wt = jnp.concatenate([w_in[:, 2 * m:o_if]]
                         + [piece for o in (o_vs, o_vw) for g in range(NSA_KV_GROUPS)
                            for piece in (w_in[:, o + g * dh:o + (g + 1) * dh], zeros(LANES - dh))]
                         + [w_in[:, o_qn:o_kc]], axis=1).T
    return w.astype(BF16), wt.astype(BF16)


def _compress_weights(cmp_pe, cmp_w1, cmp_b1, cmp_w2, cmp_b2):
    G, dh, hid = NSA_KV_GROUPS, NSA_HEAD_DIM, CMP_HIDDEN
    eye = jnp.eye(G, dtype=F32)
    w1 = cmp_w1.astype(BF16).reshape(2, CMP_BLOCK, 1, dh, hid)
    zero = jnp.zeros_like(w1)
    w1 = jnp.concatenate([jnp.concatenate([w1 if g == h else zero for h in range(G)], axis=-1)
                          for g in range(G)], axis=2)
    w1 = w1.reshape(2, CMP_BLOCK * G * dh, G * hid)
    half = CMP_STRIDE * G * dh
    wa, wb = w1[:, :half], w1[:, half:]
    pe = jnp.broadcast_to(cmp_pe[:, :, None, :], (2, CMP_BLOCK, G, dh)).reshape(2, 1, CMP_BLOCK * G * dh)
    pe = jnp.broadcast_to(pe, (2, 8, CMP_BLOCK * G * dh)).astype(BF16)
    pea, peb = pe[:, :, :half], pe[:, :, half:]
    b1 = jnp.tile(cmp_b1, (1, G)).reshape(2, 1, G * hid)
    w2 = jnp.einsum('kjd,gh->kgjhd', cmp_w2, eye).reshape(2, G * hid, G * dh).astype(BF16)
    b2 = jnp.tile(cmp_b2, (1, G))
    return wa, wb, pea, peb, b1, w2[0], b2[0].reshape(1, G * dh), w2[1].T, b2[1].reshape(G * dh, 1)


def kernel(x, c, w_ada, b_ada, norm_gains, ffn_wg, ffn_wu, ffn_wd, w_in, mlstm_conv_w, mlstm_conv_b,
           mlstm_gate_b, mlstm_head_gain, cmp_pe, cmp_w1, cmp_b1, cmp_w2, cmp_b2, w_up_mlstm, w_up_nsa, w_out):
    bsz, seq, d = x.shape
    t = bsz * seq
    h = x.reshape(t, d)
    eye = jnp.eye(NSA_WIDTH, dtype=BF16)
    for l in range(w_ada.shape[0]):
        gains = norm_gains[l]
        ada3 = _ada(c, w_ada[l], b_ada[l]).reshape(bsz, 9, d)
        wg, wu, wd = ffn_wg[l].astype(BF16), ffn_wu[l].astype(BF16), ffn_wd[l].astype(BF16)

        h = _ffn(h, ada3, gains, wg[0], wu[0], wd[0], 0, 0.5, seq)

        qk, kc, vc, kk, gt, sm, vt = _inproj(h, ada3, gains, *_reorder_w_in(w_in[l]), seq)
        gate_b_row = jnp.concatenate([mlstm_gate_b[l], jnp.zeros((LANES - _IF_COLS,), F32)]).reshape(1, LANES)
        hm = _mlstm(qk, vt, sm, _shift_matrices(min(MLSTM_L, seq), 16),
                    mlstm_conv_w[l].reshape(CONV_WIDTH, 2 * MLSTM_WIDTH), mlstm_conv_b[l].reshape(1, -1),
                    gate_b_row, mlstm_head_gain[l].reshape(-1, 1), bsz, seq)

        chunked = (bsz, seq // CMP_STRIDE, CMP_STRIDE * NSA_KV_WIDTH)
        k_cmp, v_cmp_t = _compress(kc.reshape(chunked), vc.reshape(chunked),
                                   *_compress_weights(cmp_pe[l], cmp_w1[l], cmp_b1[l], cmp_w2[l], cmp_b2[l]))
        hn = _nsa(sm, k_cmp, v_cmp_t, kk, vt, eye, bsz, seq)

        h = _merge(h, ada3, gains, hm, hn, gt, w_up_mlstm[l].astype(BF16), w_up_nsa[l].astype(BF16),
                   w_out[l].astype(BF16), seq)
        h = _ffn(h, ada3, gains, wg[1], wu[1], wd[1], 2, 0.5, seq)
    return h.reshape(bsz, seq, d)
```

```python
import functools

import jax
import jax.numpy as jnp
from jax import lax
from jax.experimental import pallas as pl
from jax.experimental.pallas import tpu as pltpu

D_MODEL = 1024
D_FF = 2816
NORM_EPS = 1e-6
CONV_WIDTH = 4
MLSTM_HEADS = 4
MLSTM_HEAD_DIM = 128
MLSTM_WIDTH = MLSTM_HEADS * MLSTM_HEAD_DIM
NSA_HEADS = 8
NSA_KV_GROUPS = 2
NSA_HPG = NSA_HEADS // NSA_KV_GROUPS
NSA_HEAD_DIM = 64
NSA_WIDTH = NSA_HEADS * NSA_HEAD_DIM
NSA_KV_WIDTH = NSA_KV_GROUPS * NSA_HEAD_DIM
CMP_BLOCK = 32
CMP_STRIDE = 16
CMP_HIDDEN = 256
SEL_BLOCK = 64
N_SELECT = 16
WINDOW = 256
FORCE_SCORE = 1e4
NEG_BIG = -1e30
M_INIT = -1e29

LANES = 128
SUBLANES = 8
VMEM_LIMIT = 56 * 1024 * 1024

FFN_SUB = 512
FFN_ROWS = 1024
FFN_FC = 256
MLSTM_L = 256
NSA_TQ = 256
SEL_UNROLL = 4

BF16 = jnp.bfloat16
F32 = jnp.float32


def _params(sem):
    return pltpu.CompilerParams(dimension_semantics=sem, vmem_limit_bytes=VMEM_LIMIT)


def _rms(y):
    return y * lax.rsqrt(jnp.mean(y * y, axis=-1, keepdims=True) + NORM_EPS)


def _sigmoid(x):
    return 0.5 * jnp.tanh(0.5 * x) + 0.5


def _dot(a, b):
    return jnp.dot(a, b, preferred_element_type=F32)


def _dot_nt(a, b, precision=None):
    return lax.dot_general(a, b, (((1,), (1,)), ((), ())), preferred_element_type=F32, precision=precision)


def _dot_tn(a, b):
    return lax.dot_general(a, b, (((0,), (0,)), ((), ())), preferred_element_type=F32)


def _ada_kernel(c_ref, w_ref, b_ref, o_ref):
    c = c_ref[...]
    sc = c * _sigmoid(c)
    o_ref[...] = jnp.dot(sc, w_ref[...], preferred_element_type=F32,
                         precision=lax.Precision.HIGHEST) + b_ref[...]


def _ada(c, w, b):
    bsz, d = c.shape
    n = w.shape[1]
    tn = 1152
    return pl.pallas_call(
        _ada_kernel,
        grid=(n // tn,),
        in_specs=[pl.BlockSpec((bsz, d), lambda j: (0, 0)),
                  pl.BlockSpec((d, tn), lambda j: (0, j)),
                  pl.BlockSpec((1, tn), lambda j: (0, j))],
        out_specs=pl.BlockSpec((bsz, tn), lambda j: (0, j)),
        out_shape=jax.ShapeDtypeStruct((bsz, n), F32),
        compiler_params=_params(("arbitrary",)),
        name="ada",
    )(c, w, b.reshape(1, n))


def _mod_rows(ada_ref, s):
    return (ada_ref[0, 3 * s:3 * s + 1, :], ada_ref[0, 3 * s + 1:3 * s + 2, :],
            ada_ref[0, 3 * s + 2:3 * s + 3, :])


def _ffn_kernel(h_ref, ada_ref, gains_ref, wg_ref, wu_ref, wd_ref, o_ref, *, s, resid):
    shift, scale, gate = _mod_rows(ada_ref, s)
    subs = [slice(r0, r0 + FFN_SUB) for r0 in range(0, h_ref.shape[0], FFN_SUB)]
    us = [(_rms(h_ref[rows, :]) * gains_ref[2 * s:2 * s + 1, :] * (1.0 + scale) + shift).astype(BF16)
          for rows in subs]
    accs = [jnp.zeros((FFN_SUB, h_ref.shape[1]), F32) for _ in subs]
    for c0 in range(0, D_FF, FFN_FC):
        for i, u in enumerate(us):
            g = _dot(u, wg_ref[:, c0:c0 + FFN_FC])
            up = _dot(u, wu_ref[:, c0:c0 + FFN_FC])
            hid = (g * _sigmoid(g) * up).astype(BF16)
            accs[i] = accs[i] + _dot(hid, wd_ref[c0:c0 + FFN_FC, :])
    for rows, acc in zip(subs, accs):
        post = _rms(acc) * gains_ref[2 * s + 1:2 * s + 2, :]
        o_ref[rows, :] = h_ref[rows, :] + resid * gate * post


def _ffn(h, ada3, gains, wg, wu, wd, s, resid, seq):
    t, d = h.shape
    tm = FFN_ROWS
    per_b = seq // tm
    const = lambda i: (0, 0)
    return pl.pallas_call(
        functools.partial(_ffn_kernel, s=s, resid=resid),
        grid=(t // tm,),
        in_specs=[pl.BlockSpec((tm, d), lambda i: (i, 0)),
                  pl.BlockSpec((1, 9, d), lambda i: (i // per_b, 0, 0)),
                  pl.BlockSpec(gains.shape, const),
                  pl.BlockSpec(wg.shape, const, pipeline_mode=pl.Buffered(1)),
                  pl.BlockSpec(wu.shape, const, pipeline_mode=pl.Buffered(1)),
                  pl.BlockSpec(wd.shape, const, pipeline_mode=pl.Buffered(1))],
        out_specs=pl.BlockSpec((tm, d), lambda i: (i, 0)),
        out_shape=jax.ShapeDtypeStruct((t, d), F32),
        compiler_params=_params(("parallel",)),
        name=f"ffn{s}",
    )(h, ada3, gains, wg, wu, wd)


_QK_W = 2 * MLSTM_WIDTH
_KC_W = NSA_KV_WIDTH
_VC_W = NSA_KV_WIDTH
_KK_W = 3 * LANES
_GT_W = 2 * D_MODEL
_SM_W = LANES
_VO_T = 2 * MLSTM_WIDTH
_VN_T = 4 * LANES
_QN_T = NSA_WIDTH
_VT_W = _VO_T + _VN_T + _QN_T
_IF_COLS = 2 * MLSTM_HEADS
_GN_COLS = 3 * NSA_HEADS
_IN_WIDTHS = (_QK_W, _KC_W, _VC_W, _KK_W, _GT_W, _SM_W)
_IN_DTYPES = (BF16, BF16, BF16, BF16, BF16, F32)


def _inproj_kernel(h_ref, ada_ref, gains_ref, w_ref, wt_ref, qk_ref, kc_ref, vc_ref, kk_ref,
                   gt_ref, sm_ref, vt_ref, *, per_b):
    tm = h_ref.shape[0]
    shift, scale, _ = _mod_rows(ada_ref, 1)
    subs = [slice(r0, r0 + FFN_SUB) for r0 in range(0, tm, FFN_SUB)]
    us = [(_rms(h_ref[rows, :]) * gains_ref[2:3, :] * (1.0 + scale) + shift).astype(BF16) for rows in subs]
    step = 512
    off = 0
    for ref in (qk_ref, kc_ref, vc_ref, kk_ref, gt_ref, sm_ref):
        width = ref.shape[1]
        for c0 in range(0, width, step):
            c1 = min(c0 + step, width)
            for rows, u in zip(subs, us):
                res = _dot(u, w_ref[:, off + c0:off + c1])
                if ref is kk_ref and c0 == 0:
                    pos = ((pl.program_id(0) % per_b) * tm + rows.start
                           + lax.broadcasted_iota(jnp.int32, (FFN_SUB, c1 - c0), 0))
                    lane = lax.broadcasted_iota(jnp.int32, (FFN_SUB, c1 - c0), 1)
                    hot = (lane == NSA_HEAD_DIM + pos // SEL_BLOCK) | (lane == LANES + pos // SEL_BLOCK)
                    res = res + jnp.where(hot, 1.0, 0.0)
                ref[rows, c0:c1] = res.astype(ref.dtype)
        off += width
    ones_row = lax.broadcasted_iota(jnp.int32, (_VN_T, FFN_SUB), 0) % LANES == NSA_HEAD_DIM
    for rows, u in zip(subs, us):
        vt_ref[0:_VO_T, rows] = _dot_nt(wt_ref[0:_VO_T, :], u).astype(vt_ref.dtype)
        vt_ref[_VO_T:_VO_T + _VN_T, rows] = (_dot_nt(wt_ref[_VO_T:_VO_T + _VN_T, :], u)
                                             + jnp.where(ones_row, 1.0, 0.0)).astype(vt_ref.dtype)
        vt_ref[_VO_T + _VN_T:, rows] = (_dot_nt(wt_ref[_VO_T + _VN_T:, :], u)
                                        * NSA_HEAD_DIM ** -0.5).astype(vt_ref.dtype)


def _inproj(h, ada3, gains, w, wt, seq):
    t, d = h.shape
    tm = FFN_ROWS
    per_b = seq // tm
    const = lambda i: (0, 0)
    return pl.pallas_call(
        functools.partial(_inproj_kernel, per_b=per_b),
        grid=(t // tm,),
        in_specs=[pl.BlockSpec((tm, d), lambda i: (i, 0)),
                  pl.BlockSpec((1, 9, d), lambda i: (i // per_b, 0, 0)),
                  pl.BlockSpec(gains.shape, const),
                  pl.BlockSpec(w.shape, const, pipeline_mode=pl.Buffered(1)),
                  pl.BlockSpec(wt.shape, const, pipeline_mode=pl.Buffered(1))],
        out_specs=[pl.BlockSpec((tm, wd), lambda i: (i, 0)) for wd in _IN_WIDTHS]
        + [pl.BlockSpec((_VT_W, tm), lambda i: (0, i))],
        out_shape=[jax.ShapeDtypeStruct((t, wd), dt) for wd, dt in zip(_IN_WIDTHS, _IN_DTYPES)]
        + [jax.ShapeDtypeStruct((_VT_W, t), BF16)],
        compiler_params=_params(("parallel",)),
        name="inproj",
    )(h, ada3, gains, w, wt)


_MX_ROWS = MLSTM_HEAD_DIM + 16


def _split3(x):
    hi = x.astype(BF16)
    r1 = x - hi.astype(F32)
    mid = r1.astype(BF16)
    return hi, mid, (r1 - mid.astype(F32)).astype(BF16)


def _mlstm_kernel(qk_ref, vot_ref, sm_ref, shift_ref, cw_ref, cb_ref, gb_ref, hg_ref, o_ref, tail, cext, mscr):
    L = qk_ref.shape[0]
    W = MLSTM_WIDTH
    dh = MLSTM_HEAD_DIM
    H = MLSTM_HEADS
    TAIL = tail.shape[0]

    @pl.when(pl.program_id(1) == 0)
    def _():
        tail[...] = jnp.zeros(tail.shape, tail.dtype)
        cext[...] = jnp.zeros(cext.shape, F32)
        mscr[...] = jnp.zeros(mscr.shape, F32)

    x = qk_ref[...]
    prev = tail[...]
    conv = cb_ref[...] + cw_ref[CONV_WIDTH - 1:CONV_WIDTH, :] * x.astype(F32)
    head = jnp.zeros((TAIL, 2 * W), F32)
    for j in range(CONV_WIDTH - 1):
        conv = conv + cw_ref[j:j + 1, :] * _dot(shift_ref[j], x)
        head = head + cw_ref[j:j + 1, :] * _dot(shift_ref[CONV_WIDTH - 1 + j, 0:TAIL, 0:TAIL], prev)
    conv = jnp.concatenate([conv[0:TAIL, :] + head, conv[TAIL:, :]], axis=0)
    tail[...] = x[L - TAIL:L, :]
    qk = conv * _sigmoid(conv)

    gts = sm_ref[...] + gb_ref[...]
    logf = jnp.minimum(gts, 0.0) - jnp.log1p(jnp.exp(-jnp.abs(gts)))
    s_idx = lax.broadcasted_iota(jnp.int32, (L, L), 0)
    t_idx = lax.broadcasted_iota(jnp.int32, (L, L), 1)
    causal = s_idx <= t_idx
    tri = jnp.where(causal, 1.0, 0.0).astype(BF16)
    logf_t = logf.T
    b_cols = sum(_dot_tn(tri, part) for part in _split3(logf))
    b_rows = sum(_dot(part, tri) for part in _split3(logf_t))
    g_rows = gts.T
    ones_tile = (lax.broadcasted_iota(jnp.int32, (_MX_ROWS - dh, L), 0) == 0).astype(BF16)

    early = []
    for h in range(H):
        q = qk[:, h * dh:(h + 1) * dh].astype(BF16)
        k = (qk[:, W + h * dh:W + (h + 1) * dh] * (dh ** -0.5)).astype(BF16)
        vext = jnp.concatenate([vot_ref[h * dh:(h + 1) * dh, :], ones_tile], axis=0)
        b_row = b_rows[H + h:H + h + 1, :]
        m_prev = mscr[h:h + 1, 0:1]
        c_old = cext[h]
        b_last = b_row[:, L - 1:L]
        g_row = b_last - b_row + g_rows[h:h + 1, :]
        m_new = jnp.maximum(b_last + m_prev, jnp.max(g_row, axis=1, keepdims=True))
        vw = (vext.astype(F32) * jnp.exp(g_row - m_new)).astype(BF16)
        early.append((vext, b_row, m_prev, _dot_nt(k, q), _dot_nt(c_old.astype(BF16), q)))
        cext[h] = jnp.exp(b_last + m_prev - m_new) * c_old + _dot(vw, k)
        mscr[h:h + 1, :] = jnp.broadcast_to(m_new, (1, LANES))

    for h, (vext, b_row, m_prev, s_kq, c_q) in enumerate(early):
        r_col = gts[:, h:h + 1] - b_cols[:, H + h:H + h + 1]
        ld = jnp.where(causal, r_col + b_row, NEG_BIG)
        a = b_row + m_prev
        m_t = jnp.maximum(a, jnp.max(ld, axis=0, keepdims=True))
        p = (s_kq * jnp.exp(ld - m_t)).astype(BF16)
        nd = _dot(vext, p) + jnp.exp(a - m_t) * c_q
        hh = nd[0:dh, :] * (1.0 / jnp.maximum(jnp.abs(nd[dh:dh + 1, :]), jnp.exp(-m_t)))

        hn = hh * lax.rsqrt(jnp.mean(hh * hh, axis=0, keepdims=True) + NORM_EPS) * hg_ref[h * dh:(h + 1) * dh, :]
        o_pre = vot_ref[W + h * dh:W + (h + 1) * dh, :].astype(F32)
        o_ref[h * dh:(h + 1) * dh, :] = (_sigmoid(o_pre) * hn).astype(o_ref.dtype)


def _mlstm(qk, vt, sm, shifts, conv_w, conv_b, gate_b_row, head_gain_col, bsz, seq):
    t = qk.shape[0]
    L = min(MLSTM_L, seq)
    nc = seq // L
    W = MLSTM_WIDTH
    const = lambda b, c: (0, 0)
    rows = lambda b, c: (b * nc + c, 0)
    cols = lambda b, c: (0, b * nc + c)
    return pl.pallas_call(
        _mlstm_kernel,
        grid=(bsz, nc),
        in_specs=[pl.BlockSpec((L, 2 * W), rows),
                  pl.BlockSpec((_VO_T, L), cols),
                  pl.BlockSpec((L, LANES), rows),
                  pl.BlockSpec(shifts.shape, lambda b, c: (0, 0, 0)),
                  pl.BlockSpec(conv_w.shape, const),
                  pl.BlockSpec(conv_b.shape, const),
                  pl.BlockSpec(gate_b_row.shape, const),
                  pl.BlockSpec(head_gain_col.shape, const)],
        out_specs=pl.BlockSpec((W, L), cols),
        out_shape=jax.ShapeDtypeStruct((W, t), BF16),
        scratch_shapes=[pltpu.VMEM((16, 2 * W), BF16),
                        pltpu.VMEM((MLSTM_HEADS, _MX_ROWS, MLSTM_HEAD_DIM), F32),
                        pltpu.VMEM((8, LANES), F32)],
        compiler_params=_params(("parallel", "arbitrary")),
        name="mlstm",
    )(qk, vt, sm, shifts, conv_w, conv_b, gate_b_row, head_gain_col)


def _shift_matrices(L, tail):
    t_idx = jnp.arange(L)[:, None]
    s_idx = jnp.arange(L)[None, :]
    mats = [(s_idx == t_idx - (CONV_WIDTH - 1 - j)) for j in range(CONV_WIDTH - 1)]
    mats += [(s_idx - tail == t_idx - (CONV_WIDTH - 1 - j)) & (s_idx < tail) & (t_idx < tail)
             for j in range(CONV_WIDTH - 1)]
    return jnp.stack(mats).astype(BF16)


def _compress_kernel(kc_ref, vc_ref, wa_ref, wb_ref, pea_ref, peb_ref, b1_ref, w2k_ref, b2k_ref, w2vt_ref, b2vt_ref,
                     ko_ref, vto_ref):
    def pre_act(i, x):
        wa, wb = wa_ref[i], wb_ref[i]
        bias = _dot(pea_ref[i], wa) + _dot(peb_ref[i], wb) + b1_ref[i]
        return _dot(x, wa) + pltpu.roll(_dot(x, wb), x.shape[0] - 1, 0) + bias[0:1, :]

    def silu(pre):
        return (pre * _sigmoid(pre)).astype(BF16)

    pre_k, pre_v = pre_act(0, kc_ref[0]), pre_act(1, vc_ref[0])
    ko_ref[0] = (_dot(silu(pre_k), w2k_ref[...]) + b2k_ref[...]).astype(ko_ref.dtype)
    vto_ref[0] = (_dot_nt(w2vt_ref[...], silu(pre_v)) + b2vt_ref[...]).astype(vto_ref.dtype)


def _compress(kc, vc, wa, wb, pea, peb, b1, w2k, b2k, w2vt, b2vt):
    bsz, n, width = kc.shape
    full = lambda a: pl.BlockSpec(a.shape, lambda b: (0,) * a.ndim)
    per_b = pl.BlockSpec((1, n, width), lambda b: (b, 0, 0))
    return pl.pallas_call(
        _compress_kernel,
        grid=(bsz,),
        in_specs=[per_b, per_b] + [full(a) for a in (wa, wb, pea, peb, b1, w2k, b2k, w2vt, b2vt)],
        out_specs=[pl.BlockSpec((1, n, NSA_KV_WIDTH), lambda b: (b, 0, 0)),
                   pl.BlockSpec((1, NSA_KV_WIDTH, n), lambda b: (b, 0, 0))],
        out_shape=[jax.ShapeDtypeStruct((bsz, n, NSA_KV_WIDTH), BF16),
                   jax.ShapeDtypeStruct((bsz, NSA_KV_WIDTH, n), BF16)],
        compiler_params=_params(("parallel",)),
        name="compress",
    )(kc, vc, wa, wb, pea, peb, b1, w2k, b2k, w2vt, b2vt)


def _softmax_cols(s):
    m = jnp.maximum(jnp.max(s, axis=0, keepdims=True), M_INIT)
    e = jnp.exp(s - m)
    return e * (1.0 / jnp.maximum(jnp.sum(e, axis=0, keepdims=True), 1e-30))


def _nsa_kernel(qt_ref, sm_ref, kc_ref, vct_ref, ks0_ref, ks1_ref, kw_ref, vs0_ref, vs1_ref, vw0_ref, vw1_ref,
                eye_ref, o_ref, imp_scr, cnt_scr, m_scr, den_scr, num_scr):
    TQ = qt_ref.shape[1]
    S = kw_ref.shape[0]
    NC = kc_ref.shape[1]
    TK = TQ
    NSEL = S // SEL_BLOCK
    NB = NSEL // SUBLANES
    HG = NSA_HPG
    DH = NSA_HEAD_DIM
    R = HG * TQ
    HQ = TQ // 2
    HR = HG * HQ
    WL = min(WINDOW + HQ, S)
    q0 = pl.program_id(1) * TQ

    qt_all = qt_ref[...]
    gates_t = _sigmoid(sm_ref[...]).T
    t_lane = q0 + lax.broadcasted_iota(jnp.int32, (1, TQ), 1)
    zeros_half = jnp.zeros((DH, R), BF16)
    sub = lax.broadcasted_iota(jnp.int32, (SUBLANES, TQ), 0)

    def lanes(f):
        xs = [f(h) for h in range(HG)]
        return jnp.concatenate([x[:, c * HQ:(c + 1) * HQ] for c in range(2) for x in xs], axis=1)

    def per_head(x):
        return lanes(lambda h: x)

    def half_heads(x):
        return jnp.concatenate([x] * HG, axis=1)

    def head_cols(x, h):
        return jnp.concatenate([x[:, (c * HG + h) * HQ:(c * HG + h + 1) * HQ] for c in range(2)], axis=1)

    def weights(s):
        m = jnp.maximum(jnp.max(s, axis=0, keepdims=True), M_INIT)
        return m, jnp.exp((s - m).astype(BF16))

    def value_sum(v_ref, k0, n, e):
        nd = _dot(v_ref[0:DH + 16, pl.ds(k0, n)], e.astype(BF16))
        return nd[0:DH, :], nd[DH:DH + 1, :]

    G = range(NSA_KV_GROUPS)
    qts = [lanes(lambda h: qt_all[(g * HG + h) * DH:(g * HG + h + 1) * DH, :]) for g in G]

    def with_half(g, other):
        return jnp.concatenate([qts[g], other] if g == 0 else [other, qts[g]], axis=0)

    def gate(g, br):
        return lanes(lambda h: gates_t[_IF_COLS + (g * HG + h) * 3 + br:
                                       _IF_COLS + (g * HG + h) * 3 + br + 1, :])

    qt_plain = [with_half(g, zeros_half) for g in G]
    halves = [(c, slice(c * HR, (c + 1) * HR)) for c in range(2)]
    w0 = [pl.multiple_of(jnp.maximum(q0 + (c + 1) * HQ - WL, 0), LANES) for c in range(2)]
    s_cmp = [_dot(kc_ref[0], qt_plain[g]) for g in G]
    s_win = [[_dot(kw_ref[pl.ds(w0[c], WL), :], qt_plain[g][:, cols]) for c, cols in halves] for g in G]
    cmp_end = lax.broadcasted_iota(jnp.int32, (NC, TQ), 0) * CMP_STRIDE + (CMP_BLOCK - 1)
    bias_c = per_head(jnp.where(cmp_end <= t_lane, 0.0, NEG_BIG))
    bias_w = []
    for c in range(2):
        wpos = w0[c] + lax.broadcasted_iota(jnp.int32, (WL, HQ), 0)
        t_half = t_lane[:, c * HQ:(c + 1) * HQ]
        bias_w.append(half_heads(jnp.where((wpos <= t_half) & (wpos > t_half - WINDOW), 0.0, NEG_BIG)))
    jj = lax.broadcasted_iota(jnp.int32, (NSEL, NC), 0) * SEL_BLOCK
    nn = lax.broadcasted_iota(jnp.int32, (NSEL, NC), 1) * CMP_STRIDE
    overlap = jnp.where((nn < jj + SEL_BLOCK) & (nn + CMP_BLOCK > jj), 1.0, 0.0).astype(BF16)
    blk = lax.broadcasted_iota(jnp.int32, (NSEL, TQ), 0)
    cur = t_lane // SEL_BLOCK
    forced = (blk == 0) | (blk == cur) | (blk == cur - 1)

    o_rest, imps = [], []
    for g in G:
        p = _softmax_cols(s_cmp[g] + bias_c)
        o_cmp = _dot(vct_ref[0], p.astype(BF16))[g * DH:(g + 1) * DH, :]
        psum = head_cols(p, 0)
        for h in range(1, HG):
            psum = psum + head_cols(p, h)
        imp = sum(_dot(overlap, part) for part in _split3(psum))
        imp = jnp.where(blk * SEL_BLOCK <= t_lane, jnp.where(forced, FORCE_SCORE, imp), -1.0)
        imp_scr[g] = imp
        imps.append(imp)
        o_win = []
        for c, _ in halves:
            _, e = weights(s_win[g][c] + bias_w[c])
            num, den = value_sum((vw0_ref, vw1_ref)[g], w0[c], WL, e)
            o_win.append(num * (1.0 / jnp.maximum(den, 1e-30)))
        o_rest.append(gate(g, 0) * o_cmp + gate(g, 2) * jnp.concatenate(o_win, axis=1))

    cnt_scr[...] = jnp.zeros(cnt_scr.shape, F32)
    for ib in range(NB):
        @pl.when((ib * SUBLANES * SEL_BLOCK < q0 + TQ) & (q0 + TQ > N_SELECT * SEL_BLOCK))
        def _():
            for g in G:
                cnt = [cnt_scr[g, jb * SUBLANES:(jb + 1) * SUBLANES, :] for jb in range(NB)]
                for u in range(SUBLANES):
                    r = imp_scr[g, ib * SUBLANES + u:ib * SUBLANES + u + 1, :]
                    for jb in range(NB):
                        x = imps[g][jb * SUBLANES:(jb + 1) * SUBLANES, :]
                        if jb > ib:
                            ahead = jnp.where(r >= x, 1.0, 0.0)
                        elif jb < ib:
                            ahead = jnp.where(r > x, 1.0, 0.0)
                        else:
                            ahead = jnp.where(r > x, 1.0, jnp.where((r == x) & (sub > u), 1.0, 0.0))
                        cnt[jb] = cnt[jb] + ahead
                for jb in range(NB):
                    cnt_scr[g, jb * SUBLANES:(jb + 1) * SUBLANES, :] = cnt[jb]

    groups = []
    for g in G:
        selb = jnp.where(cnt_scr[g] < float(min(N_SELECT, NSEL)), 0.0, NEG_BIG)
        if NSEL < DH:
            selb = jnp.concatenate([selb, jnp.zeros((DH - NSEL, TQ), F32)], axis=0)
        groups.append((with_half(g, per_head(selb.astype(BF16))), o_rest[g], gate(g, 1)))

    def sel_scores(g, k0, diagonal):
        if not diagonal:
            return [(slice(0, R), TK, _dot((ks0_ref, ks1_ref)[g][pl.ds(k0, TK), :], groups[g][0]))]
        pieces = []
        for c, cols in halves:
            n = (c + 1) * HQ
            kpos = k0 + lax.broadcasted_iota(jnp.int32, (n, HQ), 0)
            bias = half_heads(jnp.where(kpos <= t_lane[:, c * HQ:(c + 1) * HQ], 0.0, NEG_BIG))
            pieces.append((cols, n, _dot((ks0_ref, ks1_ref)[g][pl.ds(k0, n), :], groups[g][0][:, cols]) + bias))
        return pieces

    def sel_fold(g, k0, cols, n, s):
        m_t, e = weights(s)
        num_t, den_t = value_sum((vs0_ref, vs1_ref)[g], k0, n, e)
        m = m_scr[g, :, cols]
        m_new = jnp.maximum(m, m_t)
        a, b = jnp.exp(m - m_new), jnp.exp(m_t - m_new)
        m_scr[g, :, cols] = m_new
        den_scr[g, :, cols] = a * den_scr[g, :, cols] + b * den_t
        num_scr[g, :, cols] = a * num_scr[g, :, cols] + b * num_t

    def sel_block(tiles):
        scores = [[sel_scores(g, k0, diagonal) for g in G] for k0, diagonal in tiles]
        for (k0, _), per_group in zip(tiles, scores):
            for g, pieces in enumerate(per_group):
                for cols, n, s in pieces:
                    sel_fold(g, k0, cols, n, s)

    m_scr[...] = jnp.full(m_scr.shape, M_INIT, F32)
    den_scr[...] = jnp.zeros(den_scr.shape, F32)
    num_scr[...] = jnp.zeros(num_scr.shape, F32)
    n_full = q0 // TK

    def block_body(j, c):
        k0 = pl.multiple_of(j * (SEL_UNROLL * TK), SEL_UNROLL * TK)
        sel_block([(k0 + i * TK, False) for i in range(SEL_UNROLL)])
        return c

    lax.fori_loop(0, n_full // SEL_UNROLL, block_body, 0)
    k_diag = pl.multiple_of(n_full * TK, TK)

    for rem in range(SEL_UNROLL):
        @pl.when(n_full % SEL_UNROLL == rem)
        def _():
            sel_block([(pl.multiple_of(k_diag - (rem - i) * TK, TK), i == rem) for i in range(rem + 1)])

    out_rows = []
    for g, (_, o_rest, gate_sel) in enumerate(groups):
        o = o_rest + gate_sel * (num_scr[g] * (1.0 / jnp.maximum(den_scr[g], 1e-30)))
        out_rows += [head_cols(o, h) for h in range(HG)]

    o_ref[...] = _dot_tn(jnp.concatenate(out_rows, axis=0).astype(BF16), eye_ref[...]).astype(o_ref.dtype)


def _nsa(sm, k_cmp, v_cmp_t, kk, vt, eye, bsz, seq):
    t = sm.shape[0]
    tq = min(NSA_TQ, seq)
    nq = seq // tq
    nc = k_cmp.shape[1]
    nsel = seq // SEL_BLOCK
    assert nsel <= NSA_HEAD_DIM and nsel % SUBLANES == 0 and tq % (2 * LANES) == 0
    rows = lambda b, i: (b * nq + i, 0)
    const = lambda b, i: (0, 0)
    key_piece = lambda j: pl.BlockSpec((seq, LANES), lambda b, i: (b, j))
    val_piece = lambda j: pl.BlockSpec((LANES, seq), lambda b, i: (_VO_T // LANES + j, b))
    return pl.pallas_call(
        _nsa_kernel,
        grid=(bsz, nq),
        in_specs=[pl.BlockSpec((_QN_T, tq), lambda b, i: ((_VO_T + _VN_T) // _QN_T, b * nq + i)),
                  pl.BlockSpec((tq, LANES), rows),
                  pl.BlockSpec((1, nc, NSA_KV_WIDTH), lambda b, i: (b, 0, 0)),
                  pl.BlockSpec((1, NSA_KV_WIDTH, nc), lambda b, i: (b, 0, 0)),
                  key_piece(0), key_piece(1), key_piece(2),
                  val_piece(0), val_piece(1), val_piece(2), val_piece(3),
                  pl.BlockSpec(eye.shape, const)],
        out_specs=pl.BlockSpec((tq, NSA_WIDTH), rows),
        out_shape=jax.ShapeDtypeStruct((t, NSA_WIDTH), BF16),
        scratch_shapes=[pltpu.VMEM((NSA_KV_GROUPS, nsel, tq), F32), pltpu.VMEM((NSA_KV_GROUPS, nsel, tq), F32),
                        pltpu.VMEM((NSA_KV_GROUPS, 1, NSA_HPG * tq), F32),
                        pltpu.VMEM((NSA_KV_GROUPS, 1, NSA_HPG * tq), F32),
                        pltpu.VMEM((NSA_KV_GROUPS, NSA_HEAD_DIM, NSA_HPG * tq), F32)],
        compiler_params=_params(("parallel", "arbitrary")),
        name="nsa",
    )(vt, sm, k_cmp, v_cmp_t, kk, kk, kk, vt, vt, vt, vt, eye)


def _merge_kernel(h_ref, ada_ref, gains_ref, hm_ref, hn_ref, gt_ref, wm_ref, wn_ref, wo_ref, o_ref):
    d = h_ref.shape[1]
    _, _, gate = _mod_rows(ada_ref, 1)
    subs = [slice(r0, r0 + FFN_SUB) for r0 in range(0, h_ref.shape[0], FFN_SUB)]
    ys = [(_dot_tn(hm_ref[:, rows], wm_ref[...]),
           _dot(hn_ref[rows, :], wn_ref[...])) for rows in subs]
    zs = []
    for rows, (y_m, y_n) in zip(subs, ys):
        merged = (_sigmoid(gt_ref[rows, :d].astype(F32)) * y_m + _sigmoid(gt_ref[rows, d:].astype(F32)) * y_n)
        zs.append(_dot(merged.astype(BF16), wo_ref[...]))
    for rows, z in zip(subs, zs):
        o_ref[rows, :] = h_ref[rows, :] + gate * (_rms(z) * gains_ref[3:4, :])


def _merge(h, ada3, gains, hm, hn, gt, wm, wn, wo, seq):
    t, d = h.shape
    tm = FFN_ROWS
    per_b = seq // tm
    const = lambda i: (0, 0)
    rows = lambda i: (i, 0)
    return pl.pallas_call(
        _merge_kernel,
        grid=(t // tm,),
        in_specs=[pl.BlockSpec((tm, d), rows),
                  pl.BlockSpec((1, 9, d), lambda i: (i // per_b, 0, 0)),
                  pl.BlockSpec(gains.shape, const),
                  pl.BlockSpec((hm.shape[0], tm), lambda i: (0, i)),
                  pl.BlockSpec((tm, hn.shape[1]), rows),
                  pl.BlockSpec((tm, gt.shape[1]), rows),
                  pl.BlockSpec(wm.shape, const), pl.BlockSpec(wn.shape, const), pl.BlockSpec(wo.shape, const)],
        out_specs=pl.BlockSpec((tm, d), rows),
        out_shape=jax.ShapeDtypeStruct((t, d), F32),
        compiler_params=_params(("parallel",)),
        name="merge",
    )(h, ada3, gains, hm, hn, gt, wm, wn, wo)


def _reorder_w_in(w_in):
    m, kvw, dh = MLSTM_WIDTH, NSA_KV_WIDTH, NSA_HEAD_DIM
    o_if = 4 * m
    o_qn = o_if + _IF_COLS
    o_kc = o_qn + NSA_WIDTH
    o_vc, o_ks, o_vs, o_kw, o_vw = (o_kc + i * kvw for i in range(1, 6))
    o_gn = o_vw + kvw
    o_gm = o_gn + _GN_COLS
    zeros = lambda n: jnp.zeros((w_in.shape[0], n), w_in.dtype)
    w = jnp.concatenate([
        w_in[:, :2 * m], w_in[:, o_kc:o_vc], w_in[:, o_vc:o_ks],
        w_in[:, o_ks:o_ks + dh], zeros(dh), zeros(dh), w_in[:, o_ks + dh:o_vs], w_in[:, o_kw:o_vw],
        w_in[:, o_gm:],
        w_in[:, o_if:o_qn], w_in[:, o_gn:o_gm], zeros(_SM_W - _IF_COLS - _GN_COLS)], axis=1)
    wt = jnp.concatenate([w_in[:, 2 * m:o_if]]
                         + [piece for o in (o_vs, o_vw) for g in range(NSA_KV_GROUPS)
                            for piece in (w_in[:, o + g * dh:o + (g + 1) * dh], zeros(LANES - dh))]
                         + [w_in[:, o_qn:o_kc]], axis=1).T
    return w.astype(BF16), wt.astype(BF16)


def _compress_weights(cmp_pe, cmp_w1, cmp_b1, cmp_w2, cmp_b2):
    G, dh, hid = NSA_KV_GROUPS, NSA_HEAD_DIM, CMP_HIDDEN
    eye = jnp.eye(G, dtype=F32)
    w1 = cmp_w1.astype(BF16).reshape(2, CMP_BLOCK, 1, dh, hid)
    zero = jnp.zeros_like(w1)
    w1 = jnp.concatenate([jnp.concatenate([w1 if g == h else zero for h in range(G)], axis=-1)
                          for g in range(G)], axis=2)
    w1 = w1.reshape(2, CMP_BLOCK * G * dh, G * hid)
    half = CMP_STRIDE * G * dh
    wa, wb = w1[:, :half], w1[:, half:]
    pe = jnp.broadcast_to(cmp_pe[:, :, None, :], (2, CMP_BLOCK, G, dh)).reshape(2, 1, CMP_BLOCK * G * dh)
    pe = jnp.broadcast_to(pe, (2, 8, CMP_BLOCK * G * dh)).astype(BF16)
    pea, peb = pe[:, :, :half], pe[:, :, half:]
    b1 = jnp.tile(cmp_b1, (1, G)).reshape(2, 1, G * hid)
    w2 = jnp.einsum('kjd,gh->kgjhd', cmp_w2, eye).reshape(2, G * hid, G * dh).astype(BF16)
    b2 = jnp.tile(cmp_b2, (1, G))
    return wa, wb, pea, peb, b1, w2[0], b2[0].reshape(1, G * dh), w2[1].T, b2[1].reshape(G * dh, 1)


def kernel(x, c, w_ada, b_ada, norm_gains, ffn_wg, ffn_wu, ffn_wd, w_in, mlstm_conv_w, mlstm_conv_b,
           mlstm_gate_b, mlstm_head_gain, cmp_pe, cmp_w1, cmp_b1, cmp_w2, cmp_b2, w_up_mlstm, w_up_nsa, w_out):
    bsz, seq, d = x.shape
    t = bsz * seq
    h = x.reshape(t, d)
    eye = jnp.eye(NSA_WIDTH, dtype=BF16)
    for l in range(w_ada.shape[0]):
        gains = norm_gains[l]
        ada3 = _ada(c, w_ada[l], b_ada[l]).reshape(bsz, 9, d)
        wg, wu, wd = ffn_wg[l].astype(BF16), ffn_wu[l].astype(BF16), ffn_wd[l].astype(BF16)

        h = _ffn(h, ada3, gains, wg[0], wu[0], wd[0], 0, 0.5, seq)

        qk, kc, vc, kk, gt, sm, vt = _inproj(h, ada3, gains, *_reorder_w_in(w_in[l]), seq)
        gate_b_row = jnp.concatenate([mlstm_gate_b[l], jnp.zeros((LANES - _IF_COLS,), F32)]).reshape(1, LANES)
        hm = _mlstm(qk, vt, sm, _shift_matrices(min(MLSTM_L, seq), 16),
                    mlstm_conv_w[l].reshape(CONV_WIDTH, 2 * MLSTM_WIDTH), mlstm_conv_b[l].reshape(1, -1),
                    gate_b_row, mlstm_head_gain[l].reshape(-1, 1), bsz, seq)

        chunked = (bsz, seq // CMP_STRIDE, CMP_STRIDE * NSA_KV_WIDTH)
        k_cmp, v_cmp_t = _compress(kc.reshape(chunked), vc.reshape(chunked),
                                   *_compress_weights(cmp_pe[l], cmp_w1[l], cmp_b1[l], cmp_w2[l], cmp_b2[l]))
        hn = _nsa(sm, k_cmp, v_cmp_t, kk, vt, eye, bsz, seq)

        h = _merge(h, ada3, gains, hm, hn, gt, w_up_mlstm[l].astype(BF16), w_up_nsa[l].astype(BF16),
                   w_out[l].astype(BF16), seq)
        h = _ffn(h, ada3, gains, wg[1], wu[1], wd[1], 2, 0.5, seq)
    return h.reshape(bsz, seq, d)
```

```python
import functools

import jax
import jax.numpy as jnp
from jax import lax
from jax.experimental import pallas as pl
from jax.experimental.pallas import tpu as pltpu

D_MODEL = 1024
D_FF = 2816
NORM_EPS = 1e-6
CONV_WIDTH = 4
MLSTM_HEADS = 4
MLSTM_HEAD_DIM = 128
MLSTM_WIDTH = MLSTM_HEADS * MLSTM_HEAD_DIM
NSA_HEADS = 8
NSA_KV_GROUPS = 2
NSA_HPG = NSA_HEADS // NSA_KV_GROUPS
NSA_HEAD_DIM = 64
NSA_WIDTH = NSA_HEADS * NSA_HEAD_DIM
NSA_KV_WIDTH = NSA_KV_GROUPS * NSA_HEAD_DIM
CMP_BLOCK = 32
CMP_STRIDE = 16
CMP_HIDDEN = 256
SEL_BLOCK = 64
N_SELECT = 16
WINDOW = 256
FORCE_SCORE = 1e4
NEG_BIG = -1e30
M_INIT = -1e29

LANES = 128
SUBLANES = 8
VMEM_LIMIT = 56 * 1024 * 1024

FFN_SUB = 512
PROJ_SUB = 256
FFN_ROWS = 1024
FFN_FC = 256
MLSTM_L = 256
NSA_TQ = 256
SEL_UNROLL = 4

BF16 = jnp.bfloat16
F32 = jnp.float32


def _params(sem):
    return pltpu.CompilerParams(dimension_semantics=sem, vmem_limit_bytes=VMEM_LIMIT)


def _rms(y):
    return y * lax.rsqrt(jnp.mean(y * y, axis=-1, keepdims=True) + NORM_EPS)


def _sigmoid(x):
    return 0.5 * jnp.tanh(0.5 * x) + 0.5


def _dot(a, b):
    return jnp.dot(a, b, preferred_element_type=F32)


def _dot_nt(a, b, precision=None):
    return lax.dot_general(a, b, (((1,), (1,)), ((), ())), preferred_element_type=F32, precision=precision)


def _dot_tn(a, b):
    return lax.dot_general(a, b, (((0,), (0,)), ((), ())), preferred_element_type=F32)


def _ada_kernel(c_ref, w_ref, b_ref, o_ref):
    c = c_ref[...]
    sc = c * _sigmoid(c)
    o_ref[...] = jnp.dot(sc, w_ref[...], preferred_element_type=F32,
                         precision=lax.Precision.HIGHEST) + b_ref[...]


def _ada(c, w, b):
    bsz, d = c.shape
    n = w.shape[1]
    tn = 1152
    return pl.pallas_call(
        _ada_kernel,
        grid=(n // tn,),
        in_specs=[pl.BlockSpec((bsz, d), lambda j: (0, 0)),
                  pl.BlockSpec((d, tn), lambda j: (0, j)),
                  pl.BlockSpec((1, tn), lambda j: (0, j))],
        out_specs=pl.BlockSpec((bsz, tn), lambda j: (0, j)),
        out_shape=jax.ShapeDtypeStruct((bsz, n), F32),
        compiler_params=_params(("arbitrary",)),
        name="ada",
    )(c, w, b.reshape(1, n))


def _mod_rows(ada_ref, s):
    return (ada_ref[0, 3 * s:3 * s + 1, :], ada_ref[0, 3 * s + 1:3 * s + 2, :],
            ada_ref[0, 3 * s + 2:3 * s + 3, :])


def _ffn_kernel(h_ref, ada_ref, gains_ref, wg_ref, wu_ref, wd_ref, o_ref, *, s, resid):
    shift, scale, gate = _mod_rows(ada_ref, s)
    subs = [slice(r0, r0 + FFN_SUB) for r0 in range(0, h_ref.shape[0], FFN_SUB)]
    us = [(_rms(h_ref[rows, :]) * gains_ref[2 * s:2 * s + 1, :] * (1.0 + scale) + shift).astype(BF16)
          for rows in subs]
    accs = [jnp.zeros((FFN_SUB, h_ref.shape[1]), F32) for _ in subs]
    for c0 in range(0, D_FF, FFN_FC):
        for i, u in enumerate(us):
            g = _dot(u, wg_ref[:, c0:c0 + FFN_FC])
            up = _dot(u, wu_ref[:, c0:c0 + FFN_FC])
            hid = (g * _sigmoid(g) * up).astype(BF16)
            accs[i] = accs[i] + _dot(hid, wd_ref[c0:c0 + FFN_FC, :])
    for rows, acc in zip(subs, accs):
        post = _rms(acc) * gains_ref[2 * s + 1:2 * s + 2, :]
        o_ref[rows, :] = h_ref[rows, :] + resid * gate * post


def _ffn(h, ada3, gains, wg, wu, wd, s, resid, seq):
    t, d = h.shape
    tm = FFN_ROWS
    per_b = seq // tm
    const = lambda i: (0, 0)
    return pl.pallas_call(
        functools.partial(_ffn_kernel, s=s, resid=resid),
        grid=(t // tm,),
        in_specs=[pl.BlockSpec((tm, d), lambda i: (i, 0)),
                  pl.BlockSpec((1, 9, d), lambda i: (i // per_b, 0, 0)),
                  pl.BlockSpec(gains.shape, const),
                  pl.BlockSpec(wg.shape, const, pipeline_mode=pl.Buffered(1)),
                  pl.BlockSpec(wu.shape, const, pipeline_mode=pl.Buffered(1)),
                  pl.BlockSpec(wd.shape, const, pipeline_mode=pl.Buffered(1))],
        out_specs=pl.BlockSpec((tm, d), lambda i: (i, 0)),
        out_shape=jax.ShapeDtypeStruct((t, d), F32),
        compiler_params=_params(("parallel",)),
        name=f"ffn{s}",
    )(h, ada3, gains, wg, wu, wd)


_QK_W = 2 * MLSTM_WIDTH
_KC_W = NSA_KV_WIDTH
_VC_W = NSA_KV_WIDTH
_KK_W = 3 * LANES
_GT_W = 2 * D_MODEL
_SM_W = LANES
_VO_T = 2 * MLSTM_WIDTH
_VN_T = 4 * LANES
_QN_T = NSA_WIDTH
_VT_W = _VO_T + _VN_T + _QN_T
_IF_COLS = 2 * MLSTM_HEADS
_GN_COLS = 3 * NSA_HEADS
_IN_WIDTHS = (_QK_W, _KC_W, _VC_W, _KK_W, _GT_W, _SM_W)
_IN_DTYPES = (BF16, BF16, BF16, BF16, BF16, F32)


def _inproj_kernel(h_ref, ada_ref, gains_ref, w_ref, wt_ref, qk_ref, kc_ref, vc_ref, kk_ref,
                   gt_ref, sm_ref, vt_ref, *, per_b):
    tm = h_ref.shape[0]
    shift, scale, _ = _mod_rows(ada_ref, 1)
    subs = [slice(r0, r0 + PROJ_SUB) for r0 in range(0, tm, PROJ_SUB)]
    us = [(_rms(h_ref[rows, :]) * gains_ref[2:3, :] * (1.0 + scale) + shift).astype(BF16) for rows in subs]
    step = 512
    off = 0
    for ref in (qk_ref, kc_ref, vc_ref, kk_ref, gt_ref, sm_ref):
        width = ref.shape[1]
        for c0 in range(0, width, step):
            c1 = min(c0 + step, width)
            for rows, u in zip(subs, us):
                res = _dot(u, w_ref[:, off + c0:off + c1])
                if ref is kk_ref and c0 == 0:
                    pos = ((pl.program_id(0) % per_b) * tm + rows.start
                           + lax.broadcasted_iota(jnp.int32, (PROJ_SUB, c1 - c0), 0))
                    lane = lax.broadcasted_iota(jnp.int32, (PROJ_SUB, c1 - c0), 1)
                    hot = (lane == NSA_HEAD_DIM + pos // SEL_BLOCK) | (lane == LANES + pos // SEL_BLOCK)
                    res = res + jnp.where(hot, 1.0, 0.0)
                ref[rows, c0:c1] = res.astype(ref.dtype)
        off += width
    ones_row = lax.broadcasted_iota(jnp.int32, (_VN_T, PROJ_SUB), 0) % LANES == NSA_HEAD_DIM
    for rows, u in zip(subs, us):
        vt_ref[0:_VO_T, rows] = _dot_nt(wt_ref[0:_VO_T, :], u).astype(vt_ref.dtype)
        vt_ref[_VO_T:_VO_T + _VN_T, rows] = (_dot_nt(wt_ref[_VO_T:_VO_T + _VN_T, :], u)
                                             + jnp.where(ones_row, 1.0, 0.0)).astype(vt_ref.dtype)
        vt_ref[_VO_T + _VN_T:, rows] = (_dot_nt(wt_ref[_VO_T + _VN_T:, :], u)
                                        * NSA_HEAD_DIM ** -0.5).astype(vt_ref.dtype)


def _inproj(h, ada3, gains, w, wt, seq):
    t, d = h.shape
    tm = FFN_ROWS
    per_b = seq // tm
    const = lambda i: (0, 0)
    return pl.pallas_call(
        functools.partial(_inproj_kernel, per_b=per_b),
        grid=(t // tm,),
        in_specs=[pl.BlockSpec((tm, d), lambda i: (i, 0)),
                  pl.BlockSpec((1, 9, d), lambda i: (i // per_b, 0, 0)),
                  pl.BlockSpec(gains.shape, const),
                  pl.BlockSpec(w.shape, const, pipeline_mode=pl.Buffered(1)),
                  pl.BlockSpec(wt.shape, const, pipeline_mode=pl.Buffered(1))],
        out_specs=[pl.BlockSpec((tm, wd), lambda i: (i, 0)) for wd in _IN_WIDTHS]
        + [pl.BlockSpec((_VT_W, tm), lambda i: (0, i))],
        out_shape=[jax.ShapeDtypeStruct((t, wd), dt) for wd, dt in zip(_IN_WIDTHS, _IN_DTYPES)]
        + [jax.ShapeDtypeStruct((_VT_W, t), BF16)],
        compiler_params=_params(("parallel",)),
        name="inproj",
    )(h, ada3, gains, w, wt)


_MX_ROWS = MLSTM_HEAD_DIM + 16


def _split3(x):
    hi = x.astype(BF16)
    r1 = x - hi.astype(F32)
    mid = r1.astype(BF16)
    return hi, mid, (r1 - mid.astype(F32)).astype(BF16)


def _mlstm_kernel(qk_ref, vot_ref, sm_ref, shift_ref, cw_ref, cb_ref, gb_ref, hg_ref, o_ref, tail, cext, mscr):
    L = qk_ref.shape[0]
    W = MLSTM_WIDTH
    dh = MLSTM_HEAD_DIM
    H = MLSTM_HEADS
    TAIL = tail.shape[0]

    @pl.when(pl.program_id(1) == 0)
    def _():
        tail[...] = jnp.zeros(tail.shape, tail.dtype)
        cext[...] = jnp.zeros(cext.shape, F32)
        mscr[...] = jnp.zeros(mscr.shape, F32)

    x = qk_ref[...]
    prev = tail[...]
    conv = cb_ref[...] + cw_ref[CONV_WIDTH - 1:CONV_WIDTH, :] * x.astype(F32)
    head = jnp.zeros((TAIL, 2 * W), F32)
    for j in range(CONV_WIDTH - 1):
        conv = conv + cw_ref[j:j + 1, :] * _dot(shift_ref[j], x)
        head = head + cw_ref[j:j + 1, :] * _dot(shift_ref[CONV_WIDTH - 1 + j, 0:TAIL, 0:TAIL], prev)
    conv = jnp.concatenate([conv[0:TAIL, :] + head, conv[TAIL:, :]], axis=0)
    tail[...] = x[L - TAIL:L, :]
    qk = conv * _sigmoid(conv)

    gts = sm_ref[...] + gb_ref[...]
    logf = jnp.minimum(gts, 0.0) - jnp.log1p(jnp.exp(-jnp.abs(gts)))
    s_idx = lax.broadcasted_iota(jnp.int32, (L, L), 0)
    t_idx = lax.broadcasted_iota(jnp.int32, (L, L), 1)
    causal = s_idx <= t_idx
    tri = jnp.where(causal, 1.0, 0.0).astype(BF16)
    logf_t = logf.T
    b_cols = sum(_dot_tn(tri, part) for part in _split3(logf))
    b_rows = sum(_dot(part, tri) for part in _split3(logf_t))
    g_rows = gts.T
    ones_tile = (lax.broadcasted_iota(jnp.int32, (_MX_ROWS - dh, L), 0) == 0).astype(BF16)

    for h in range(H):
        q = qk[:, h * dh:(h + 1) * dh].astype(BF16)
        k = (qk[:, W + h * dh:W + (h + 1) * dh] * (dh ** -0.5)).astype(BF16)
        vext = jnp.concatenate([vot_ref[h * dh:(h + 1) * dh, :], ones_tile], axis=0)
        b_row = b_rows[H + h:H + h + 1, :]
        i_row = g_rows[h:h + 1, :]
        r_col = gts[:, h:h + 1] - b_cols[:, H + h:H + h + 1]
        m_prev = mscr[h:h + 1, 0:1]
        c_old = cext[h]

        ld = jnp.where(causal, r_col + b_row, NEG_BIG)
        a = b_row + m_prev
        m_t = jnp.maximum(a, jnp.max(ld, axis=0, keepdims=True))
        p = (_dot_nt(k, q) * jnp.exp(ld - m_t)).astype(BF16)
        w_inter = jnp.exp(a - m_t)
        nd = _dot(vext, p) + w_inter * _dot_nt(c_old.astype(BF16), q)
        hh = nd[0:dh, :] * (1.0 / jnp.maximum(jnp.abs(nd[dh:dh + 1, :]), jnp.exp(-m_t)))

        b_last = b_row[:, L - 1:L]
        g_row = b_last - b_row + i_row
        m_new = jnp.maximum(b_last + m_prev, jnp.max(g_row, axis=1, keepdims=True))
        vw = (vext.astype(F32) * jnp.exp(g_row - m_new)).astype(BF16)
        cext[h] = jnp.exp(b_last + m_prev - m_new) * c_old + _dot(vw, k)
        mscr[h:h + 1, :] = jnp.broadcast_to(m_new, (1, LANES))

        hn = hh * lax.rsqrt(jnp.mean(hh * hh, axis=0, keepdims=True) + NORM_EPS) * hg_ref[h * dh:(h + 1) * dh, :]
        o_pre = vot_ref[W + h * dh:W + (h + 1) * dh, :].astype(F32)
        o_ref[h * dh:(h + 1) * dh, :] = (_sigmoid(o_pre) * hn).astype(o_ref.dtype)


def _mlstm(qk, vt, sm, shifts, conv_w, conv_b, gate_b_row, head_gain_col, bsz, seq):
    t = qk.shape[0]
    L = min(MLSTM_L, seq)
    nc = seq // L
    W = MLSTM_WIDTH
    const = lambda b, c: (0, 0)
    rows = lambda b, c: (b * nc + c, 0)
    cols = lambda b, c: (0, b * nc + c)
    return pl.pallas_call(
        _mlstm_kernel,
        grid=(bsz, nc),
        in_specs=[pl.BlockSpec((L, 2 * W), rows),
                  pl.BlockSpec((_VO_T, L), cols),
                  pl.BlockSpec((L, LANES), rows),
                  pl.BlockSpec(shifts.shape, lambda b, c: (0, 0, 0)),
                  pl.BlockSpec(conv_w.shape, const),
                  pl.BlockSpec(conv_b.shape, const),
                  pl.BlockSpec(gate_b_row.shape, const),
                  pl.BlockSpec(head_gain_col.shape, const)],
        out_specs=pl.BlockSpec((W, L), cols),
        out_shape=jax.ShapeDtypeStruct((W, t), BF16),
        scratch_shapes=[pltpu.VMEM((16, 2 * W), BF16),
                        pltpu.VMEM((MLSTM_HEADS, _MX_ROWS, MLSTM_HEAD_DIM), F32),
                        pltpu.VMEM((8, LANES), F32)],
        compiler_params=_params(("parallel", "arbitrary")),
        name="mlstm",
    )(qk, vt, sm, shifts, conv_w, conv_b, gate_b_row, head_gain_col)


def _shift_matrices(L, tail):
    t_idx = jnp.arange(L)[:, None]
    s_idx = jnp.arange(L)[None, :]
    mats = [(s_idx == t_idx - (CONV_WIDTH - 1 - j)) for j in range(CONV_WIDTH - 1)]
    mats += [(s_idx - tail == t_idx - (CONV_WIDTH - 1 - j)) & (s_idx < tail) & (t_idx < tail)
             for j in range(CONV_WIDTH - 1)]
    return jnp.stack(mats).astype(BF16)


def _compress_kernel(kc_ref, vc_ref, wa_ref, wb_ref, pea_ref, peb_ref, b1_ref, w2k_ref, b2k_ref, w2vt_ref, b2vt_ref,
                     ko_ref, vto_ref):
    def pre_act(i, x):
        wa, wb = wa_ref[i], wb_ref[i]
        bias = _dot(pea_ref[i], wa) + _dot(peb_ref[i], wb) + b1_ref[i]
        return _dot(x, wa) + pltpu.roll(_dot(x, wb), x.shape[0] - 1, 0) + bias[0:1, :]

    def silu(pre):
        return (pre * _sigmoid(pre)).astype(BF16)

    pre_k, pre_v = pre_act(0, kc_ref[0]), pre_act(1, vc_ref[0])
    ko_ref[0] = (_dot(silu(pre_k), w2k_ref[...]) + b2k_ref[...]).astype(ko_ref.dtype)
    vto_ref[0] = (_dot_nt(w2vt_ref[...], silu(pre_v)) + b2vt_ref[...]).astype(vto_ref.dtype)


def _compress(kc, vc, wa, wb, pea, peb, b1, w2k, b2k, w2vt, b2vt):
    bsz, n, width = kc.shape
    full = lambda a: pl.BlockSpec(a.shape, lambda b: (0,) * a.ndim)
    per_b = pl.BlockSpec((1, n, width), lambda b: (b, 0, 0))
    return pl.pallas_call(
        _compress_kernel,
        grid=(bsz,),
        in_specs=[per_b, per_b] + [full(a) for a in (wa, wb, pea, peb, b1, w2k, b2k, w2vt, b2vt)],
        out_specs=[pl.BlockSpec((1, n, NSA_KV_WIDTH), lambda b: (b, 0, 0)),
                   pl.BlockSpec((1, NSA_KV_WIDTH, n), lambda b: (b, 0, 0))],
        out_shape=[jax.ShapeDtypeStruct((bsz, n, NSA_KV_WIDTH), BF16),
                   jax.ShapeDtypeStruct((bsz, NSA_KV_WIDTH, n), BF16)],
        compiler_params=_params(("parallel",)),
        name="compress",
    )(kc, vc, wa, wb, pea, peb, b1, w2k, b2k, w2vt, b2vt)


def _softmax_cols(s):
    m = jnp.maximum(jnp.max(s, axis=0, keepdims=True), M_INIT)
    e = jnp.exp(s - m)
    return e * (1.0 / jnp.maximum(jnp.sum(e, axis=0, keepdims=True), 1e-30))


def _nsa_kernel(qt_ref, sm_ref, kc_ref, vct_ref, ks0_ref, ks1_ref, kw_ref, vs0_ref, vs1_ref, vw0_ref, vw1_ref,
                eye_ref, o_ref, imp_scr, cnt_scr, m_scr, den_scr, num_scr):
    TQ = qt_ref.shape[1]
    S = kw_ref.shape[0]
    NC = kc_ref.shape[1]
    TK = TQ
    NSEL = S // SEL_BLOCK
    NB = NSEL // SUBLANES
    HG = NSA_HPG
    DH = NSA_HEAD_DIM
    R = HG * TQ
    HQ = TQ // 2
    HR = HG * HQ
    WL = min(WINDOW + HQ, S)
    q0 = pl.program_id(1) * TQ

    qt_all = qt_ref[...]
    gates_t = _sigmoid(sm_ref[...]).T
    t_lane = q0 + lax.broadcasted_iota(jnp.int32, (1, TQ), 1)
    zeros_half = jnp.zeros((DH, R), BF16)
    sub = lax.broadcasted_iota(jnp.int32, (SUBLANES, TQ), 0)

    def lanes(f):
        xs = [f(h) for h in range(HG)]
        return jnp.concatenate([x[:, c * HQ:(c + 1) * HQ] for c in range(2) for x in xs], axis=1)

    def per_head(x):
        return lanes(lambda h: x)

    def half_heads(x):
        return jnp.concatenate([x] * HG, axis=1)

    def head_cols(x, h):
        return jnp.concatenate([x[:, (c * HG + h) * HQ:(c * HG + h + 1) * HQ] for c in range(2)], axis=1)

    def weights(s):
        m = jnp.maximum(jnp.max(s, axis=0, keepdims=True), M_INIT)
        return m, jnp.exp((s - m).astype(BF16))

    def value_sum(v_ref, k0, n, e):
        nd = _dot(v_ref[0:DH + 16, pl.ds(k0, n)], e.astype(BF16))
        return nd[0:DH, :], nd[DH:DH + 1, :]

    G = range(NSA_KV_GROUPS)
    qts = [lanes(lambda h: qt_all[(g * HG + h) * DH:(g * HG + h + 1) * DH, :]) for g in G]

    def with_half(g, other):
        return jnp.concatenate([qts[g], other] if g == 0 else [other, qts[g]], axis=0)

    def gate(g, br):
        return lanes(lambda h: gates_t[_IF_COLS + (g * HG + h) * 3 + br:
                                       _IF_COLS + (g * HG + h) * 3 + br + 1, :])

    qt_plain = [with_half(g, zeros_half) for g in G]
    halves = [(c, slice(c * HR, (c + 1) * HR)) for c in range(2)]
    w0 = [pl.multiple_of(jnp.maximum(q0 + (c + 1) * HQ - WL, 0), LANES) for c in range(2)]
    s_cmp = [_dot(kc_ref[0], qt_plain[g]) for g in G]
    s_win = [[_dot(kw_ref[pl.ds(w0[c], WL), :], qt_plain[g][:, cols]) for c, cols in halves] for g in G]
    cmp_end = lax.broadcasted_iota(jnp.int32, (NC, TQ), 0) * CMP_STRIDE + (CMP_BLOCK - 1)
    bias_c = per_head(jnp.where(cmp_end <= t_lane, 0.0, NEG_BIG))
    bias_w = []
    for c in range(2):
        wpos = w0[c] + lax.broadcasted_iota(jnp.int32, (WL, HQ), 0)
        t_half = t_lane[:, c * HQ:(c + 1) * HQ]
        bias_w.append(half_heads(jnp.where((wpos <= t_half) & (wpos > t_half - WINDOW), 0.0, NEG_BIG)))
    jj = lax.broadcasted_iota(jnp.int32, (NSEL, NC), 0) * SEL_BLOCK
    nn = lax.broadcasted_iota(jnp.int32, (NSEL, NC), 1) * CMP_STRIDE
    overlap = jnp.where((nn < jj + SEL_BLOCK) & (nn + CMP_BLOCK > jj), 1.0, 0.0).astype(BF16)
    blk = lax.broadcasted_iota(jnp.int32, (NSEL, TQ), 0)
    cur = t_lane // SEL_BLOCK
    forced = (blk == 0) | (blk == cur) | (blk == cur - 1)

    o_rest, imps = [], []
    for g in G:
        p = _softmax_cols(s_cmp[g] + bias_c)
        o_cmp = _dot(vct_ref[0], p.astype(BF16))[g * DH:(g + 1) * DH, :]
        psum = head_cols(p, 0)
        for h in range(1, HG):
            psum = psum + head_cols(p, h)
        imp = sum(_dot(overlap, part) for part in _split3(psum))
        imp = jnp.where(blk * SEL_BLOCK <= t_lane, jnp.where(forced, FORCE_SCORE, imp), -1.0)
        imp_scr[g] = imp
        imps.append(imp)
        o_win = []
        for c, _ in halves:
            _, e = weights(s_win[g][c] + bias_w[c])
            num, den = value_sum((vw0_ref, vw1_ref)[g], w0[c], WL, e)
            o_win.append(num * (1.0 / jnp.maximum(den, 1e-30)))
        o_rest.append(gate(g, 0) * o_cmp + gate(g, 2) * jnp.concatenate(o_win, axis=1))

    cnt_scr[...] = jnp.zeros(cnt_scr.shape, F32)
    for ib in range(NB):
        @pl.when((ib * SUBLANES * SEL_BLOCK < q0 + TQ) & (q0 + TQ > N_SELECT * SEL_BLOCK))
        def _():
            for g in G:
                cnt = [cnt_scr[g, jb * SUBLANES:(jb + 1) * SUBLANES, :] for jb in range(NB)]
                for u in range(SUBLANES):
                    r = imp_scr[g, ib * SUBLANES + u:ib * SUBLANES + u + 1, :]
                    for jb in range(NB):
                        x = imps[g][jb * SUBLANES:(jb + 1) * SUBLANES, :]
                        if jb > ib:
                            ahead = jnp.where(r >= x, 1.0, 0.0)
                        elif jb < ib:
                            ahead = jnp.where(r > x, 1.0, 0.0)
                        else:
                            ahead = jnp.where(r > x, 1.0, jnp.where((r == x) & (sub > u), 1.0, 0.0))
                        cnt[jb] = cnt[jb] + ahead
                for jb in range(NB):
                    cnt_scr[g, jb * SUBLANES:(jb + 1) * SUBLANES, :] = cnt[jb]

    groups = []
    for g in G:
        selb = jnp.where(cnt_scr[g] < float(min(N_SELECT, NSEL)), 0.0, NEG_BIG)
        if NSEL < DH:
            selb = jnp.concatenate([selb, jnp.zeros((DH - NSEL, TQ), F32)], axis=0)
        groups.append((with_half(g, per_head(selb.astype(BF16))), o_rest[g], gate(g, 1)))

    def sel_scores(g, k0, diagonal):
        if not diagonal:
            return [(slice(0, R), TK, _dot((ks0_ref, ks1_ref)[g][pl.ds(k0, TK), :], groups[g][0]))]
        pieces = []
        for c, cols in halves:
            n = (c + 1) * HQ
            kpos = k0 + lax.broadcasted_iota(jnp.int32, (n, HQ), 0)
            bias = half_heads(jnp.where(kpos <= t_lane[:, c * HQ:(c + 1) * HQ], 0.0, NEG_BIG))
            pieces.append((cols, n, _dot((ks0_ref, ks1_ref)[g][pl.ds(k0, n), :], groups[g][0][:, cols]) + bias))
        return pieces

    def sel_fold(g, k0, cols, n, s):
        m_t, e = weights(s)
        num_t, den_t = value_sum((vs0_ref, vs1_ref)[g], k0, n, e)
        m = m_scr[g, :, cols]
        m_new = jnp.maximum(m, m_t)
        a, b = jnp.exp(m - m_new), jnp.exp(m_t - m_new)
        m_scr[g, :, cols] = m_new
        den_scr[g, :, cols] = a * den_scr[g, :, cols] + b * den_t
        num_scr[g, :, cols] = a * num_scr[g, :, cols] + b * num_t

    def sel_block(tiles):
        scores = [[sel_scores(g, k0, diagonal) for g in G] for k0, diagonal in tiles]
        for (k0, _), per_group in zip(tiles, scores):
            for g, pieces in enumerate(per_group):
                for cols, n, s in pieces:
                    sel_fold(g, k0, cols, n, s)

    m_scr[...] = jnp.full(m_scr.shape, M_INIT, F32)
    den_scr[...] = jnp.zeros(den_scr.shape, F32)
    num_scr[...] = jnp.zeros(num_scr.shape, F32)
    n_full = q0 // TK

    def block_body(j, c):
        k0 = pl.multiple_of(j * (SEL_UNROLL * TK), SEL_UNROLL * TK)
        sel_block([(k0 + i * TK, False) for i in range(SEL_UNROLL)])
        return c

    lax.fori_loop(0, n_full // SEL_UNROLL, block_body, 0)
    k_diag = pl.multiple_of(n_full * TK, TK)

    for rem in range(SEL_UNROLL):
        @pl.when(n_full % SEL_UNROLL == rem)
        def _():
            sel_block([(pl.multiple_of(k_diag - (rem - i) * TK, TK), i == rem) for i in range(rem + 1)])

    out_rows = []
    for g, (_, o_rest, gate_sel) in enumerate(groups):
        o = o_rest + gate_sel * (num_scr[g] * (1.0 / jnp.maximum(den_scr[g], 1e-30)))
        out_rows += [head_cols(o, h) for h in range(HG)]

    o_ref[...] = _dot_tn(jnp.concatenate(out_rows, axis=0).astype(BF16), eye_ref[...]).astype(o_ref.dtype)


def _nsa(sm, k_cmp, v_cmp_t, kk, vt, eye, bsz, seq):
    t = sm.shape[0]
    tq = min(NSA_TQ, seq)
    nq = seq // tq
    nc = k_cmp.shape[1]
    nsel = seq // SEL_BLOCK
    assert nsel <= NSA_HEAD_DIM and nsel % SUBLANES == 0 and tq % (2 * LANES) == 0
    rows = lambda b, i: (b * nq + i, 0)
    const = lambda b, i: (0, 0)
    key_piece = lambda j: pl.BlockSpec((seq, LANES), lambda b, i: (b, j))
    val_piece = lambda j: pl.BlockSpec((LANES, seq), lambda b, i: (_VO_T // LANES + j, b))
    return pl.pallas_call(
        _nsa_kernel,
        grid=(bsz, nq),
        in_specs=[pl.BlockSpec((_QN_T, tq), lambda b, i: ((_VO_T + _VN_T) // _QN_T, b * nq + i)),
                  pl.BlockSpec((tq, LANES), rows),
                  pl.BlockSpec((1, nc, NSA_KV_WIDTH), lambda b, i: (b, 0, 0)),
                  pl.BlockSpec((1, NSA_KV_WIDTH, nc), lambda b, i: (b, 0, 0)),
                  key_piece(0), key_piece(1), key_piece(2),
                  val_piece(0), val_piece(1), val_piece(2), val_piece(3),
                  pl.BlockSpec(eye.shape, const)],
        out_specs=pl.BlockSpec((tq, NSA_WIDTH), rows),
        out_shape=jax.ShapeDtypeStruct((t, NSA_WIDTH), BF16),
        scratch_shapes=[pltpu.VMEM((NSA_KV_GROUPS, nsel, tq), F32), pltpu.VMEM((NSA_KV_GROUPS, nsel, tq), F32),
                        pltpu.VMEM((NSA_KV_GROUPS, 1, NSA_HPG * tq), F32),
                        pltpu.VMEM((NSA_KV_GROUPS, 1, NSA_HPG * tq), F32),
                        pltpu.VMEM((NSA_KV_GROUPS, NSA_HEAD_DIM, NSA_HPG * tq), F32)],
        compiler_params=_params(("parallel", "arbitrary")),
        name="nsa",
    )(vt, sm, k_cmp, v_cmp_t, kk, kk, kk, vt, vt, vt, vt, eye)


def _merge_kernel(h_ref, ada_ref, gains_ref, hm_ref, hn_ref, gt_ref, wm_ref, wn_ref, wo_ref, o_ref):
    d = h_ref.shape[1]
    _, _, gate = _mod_rows(ada_ref, 1)
    subs = [slice(r0, r0 + FFN_SUB) for r0 in range(0, h_ref.shape[0], FFN_SUB)]
    ys = [(_dot_tn(hm_ref[:, rows], wm_ref[...]),
           _dot(hn_ref[rows, :], wn_ref[...])) for rows in subs]
    zs = []
    for rows, (y_m, y_n) in zip(subs, ys):
        merged = (_sigmoid(gt_ref[rows, :d].astype(F32)) * y_m + _sigmoid(gt_ref[rows, d:].astype(F32)) * y_n)
        zs.append(_dot(merged.astype(BF16), wo_ref[...]))
    for rows, z in zip(subs, zs):
        o_ref[rows, :] = h_ref[rows, :] + gate * (_rms(z) * gains_ref[3:4, :])


def _merge(h, ada3, gains, hm, hn, gt, wm, wn, wo, seq):
    t, d = h.shape
    tm = FFN_ROWS
    per_b = seq // tm
    const = lambda i: (0, 0)
    rows = lambda i: (i, 0)
    return pl.pallas_call(
        _merge_kernel,
        grid=(t // tm,),
        in_specs=[pl.BlockSpec((tm, d), rows),
                  pl.BlockSpec((1, 9, d), lambda i: (i // per_b, 0, 0)),
                  pl.BlockSpec(gains.shape, const),
                  pl.BlockSpec((hm.shape[0], tm), lambda i: (0, i)),
                  pl.BlockSpec((tm, hn.shape[1]), rows),
                  pl.BlockSpec((tm, gt.shape[1]), rows),
                  pl.BlockSpec(wm.shape, const), pl.BlockSpec(wn.shape, const), pl.BlockSpec(wo.shape, const)],
        out_specs=pl.BlockSpec((tm, d), rows),
        out_shape=jax.ShapeDtypeStruct((t, d), F32),
        compiler_params=_params(("parallel",)),
        name="merge",
    )(h, ada3, gains, hm, hn, gt, wm, wn, wo)


def _reorder_w_in(w_in):
    m, kvw, dh = MLSTM_WIDTH, NSA_KV_WIDTH, NSA_HEAD_DIM
    o_if = 4 * m
    o_qn = o_if + _IF_COLS
    o_kc = o_qn + NSA_WIDTH
    o_vc, o_ks, o_vs, o_kw, o_vw = (o_kc + i * kvw for i in range(1, 6))
    o_gn = o_vw + kvw
    o_gm = o_gn + _GN_COLS
    zeros = lambda n: jnp.zeros((w_in.shape[0], n), w_in.dtype)
    w = jnp.concatenate([
        w_in[:, :2 * m], w_in[:, o_kc:o_vc], w_in[:, o_vc:o_ks],
        w_in[:, o_ks:o_ks + dh], zeros(dh), zeros(dh), w_in[:, o_ks + dh:o_vs], w_in[:, o_kw:o_vw],
        w_in[:, o_gm:],
        w_in[:, o_if:o_qn], w_in[:, o_gn:o_gm], zeros(_SM_W - _IF_COLS - _GN_COLS)], axis=1)
    wt = jnp.concatenate([w_in[:, 2 * m:o_if]]
                         + [piece for o in (o_vs, o_vw) for g in range(NSA_KV_GROUPS)
                            for piece in (w_in[:, o + g * dh:o + (g + 1) * dh], zeros(LANES - dh))]
                         + [w_in[:, o_qn:o_kc]], axis=1).T
    return w.astype(BF16), wt.astype(BF16)


def _compress_weights(cmp_pe, cmp_w1, cmp_b1, cmp_w2, cmp_b2):
    G, dh, hid = NSA_KV_GROUPS, NSA_HEAD_DIM, CMP_HIDDEN
    eye = jnp.eye(G, dtype=F32)
    w1 = cmp_w1.astype(BF16).reshape(2, CMP_BLOCK, 1, dh, hid)
    zero = jnp.zeros_like(w1)
    w1 = jnp.concatenate([jnp.concatenate([w1 if g == h else zero for h in range(G)], axis=-1)
                          for g in range(G)], axis=2)
    w1 = w1.reshape(2, CMP_BLOCK * G * dh, G * hid)
    half = CMP_STRIDE * G * dh
    wa, wb = w1[:, :half], w1[:, half:]
    pe = jnp.broadcast_to(cmp_pe[:, :, None, :], (2, CMP_BLOCK, G, dh)).reshape(2, 1, CMP_BLOCK * G * dh)
    pe = jnp.broadcast_to(pe, (2, 8, CMP_BLOCK * G * dh)).astype(BF16)
    pea, peb = pe[:, :, :half], pe[:, :, half:]
    b1 = jnp.tile(cmp_b1, (1, G)).reshape(2, 1, G * hid)
    w2 = jnp.einsum('kjd,gh->kgjhd', cmp_w2, eye).reshape(2, G * hid, G * dh).astype(BF16)
    b2 = jnp.tile(cmp_b2, (1, G))
    return wa, wb, pea, peb, b1, w2[0], b2[0].reshape(1, G * dh), w2[1].T, b2[1].reshape(G * dh, 1)


def kernel(x, c, w_ada, b_ada, norm_gains, ffn_wg, ffn_wu, ffn_wd, w_in, mlstm_conv_w, mlstm_conv_b,
           mlstm_gate_b, mlstm_head_gain, cmp_pe, cmp_w1, cmp_b1, cmp_w2, cmp_b2, w_up_mlstm, w_up_nsa, w_out):
    bsz, seq, d = x.shape
    t = bsz * seq
    h = x.reshape(t, d)
    eye = jnp.eye(NSA_WIDTH, dtype=BF16)
    for l in range(w_ada.shape[0]):
        gains = norm_gains[l]
        ada3 = _ada(c, w_ada[l], b_ada[l]).reshape(bsz, 9, d)
        wg, wu, wd = ffn_wg[l].astype(BF16), ffn_wu[l].astype(BF16), ffn_wd[l].astype(BF16)

        h = _ffn(h, ada3, gains, wg[0], wu[0], wd[0], 0, 0.5, seq)

        qk, kc, vc, kk, gt, sm, vt = _inproj(h, ada3, gains, *_reorder_w_in(w_in[l]), seq)
        gate_b_row = jnp.concatenate([mlstm_gate_b[l], jnp.zeros((LANES - _IF_COLS,), F32)]).reshape(1, LANES)
        hm = _mlstm(qk, vt, sm, _shift_matrices(min(MLSTM_L, seq), 16),
                    mlstm_conv_w[l].reshape(CONV_WIDTH, 2 * MLSTM_WIDTH), mlstm_conv_b[l].reshape(1, -1),
                    gate_b_row, mlstm_head_gain[l].reshape(-1, 1), bsz, seq)

        chunked = (bsz, seq // CMP_STRIDE, CMP_STRIDE * NSA_KV_WIDTH)
        k_cmp, v_cmp_t = _compress(kc.reshape(chunked), vc.reshape(chunked),
                                   *_compress_weights(cmp_pe[l], cmp_w1[l], cmp_b1[l], cmp_w2[l], cmp_b2[l]))
        hn = _nsa(sm, k_cmp, v_cmp_t, kk, vt, eye, bsz, seq)

        h = _merge(h, ada3, gains, hm, hn, gt, w_up_mlstm[l].astype(BF16), w_up_nsa[l].astype(BF16),
                   w_out[l].astype(BF16), seq)
        h = _ffn(h, ada3, gains, wg[1], wu[1], wd[1], 2, 0.5, seq)
    return h.reshape(bsz, seq, d)
```
